```python
import math
import jax, jax.numpy as jnp
from jax import lax
import numpy as np

D_MODEL = 1024
BATCH = 4
SEQ = 8192
DEPTH = 2

PLE_DIM = 256
D_MIX_SSD = D_MODEL
D_MIX_CONV = D_MODEL
D_MIX = D_MIX_SSD + D_MIX_CONV
SSD_HEAD_DIM = 64
SSD_HEADS = D_MIX_SSD // SSD_HEAD_DIM
SSD_GROUPS = 2
HEADS_PER_GROUP = SSD_HEADS // SSD_GROUPS
D_STATE = 64
SSD_CONV_W = 4
CHUNK = 128
SSD_CONV_DIM = D_MIX_SSD + 2 * SSD_GROUPS * D_STATE
SC_CONV_W = 3
IN_SPLITS = [SSD_CONV_DIM, SSD_HEADS, D_MIX_SSD, D_MIX_CONV, D_MIX_CONV, D_MIX_CONV, D_MIX_CONV]
IN_COLS = sum(IN_SPLITS)
EPS = 1e-6

kernel_name = "hybrid_ssd_shortconv_parallel_heads"


def rmsnorm(x, w):
    xf = x.astype(jnp.float32)
    xf = xf * lax.rsqrt(jnp.mean(xf * xf, axis=-1, keepdims=True) + EPS)
    return (xf * w.astype(jnp.float32)).astype(x.dtype)


def causal_dwconv(u, w, b=None):
    k_w = w.shape[0]
    length = u.shape[1]
    up = jnp.pad(u, ((0, 0), (k_w - 1, 0), (0, 0)))
    y = up[:, 0:length] * w[0]
    for k in range(1, k_w):
        y = y + up[:, k:k + length] * w[k]
    if b is not None:
        y = y + b
    return y


def segsum(a):
    t = a.shape[-1]
    cs = jnp.cumsum(a, axis=-1)
    diff = cs[..., :, None] - cs[..., None, :]
    mask = jnp.tril(jnp.ones((t, t), dtype=bool))
    return jnp.where(mask, diff, -jnp.inf)


def ssd_chunked(x, dt, a, b_mat, c_mat, d_skip):
    bsz, length, _, _ = x.shape
    nc = length // CHUNK
    g, r, p, n = SSD_GROUPS, HEADS_PER_GROUP, SSD_HEAD_DIM, D_STATE
    xd = (x * dt[..., None]).reshape(bsz, nc, CHUNK, g, r, p)
    bc = b_mat.reshape(bsz, nc, CHUNK, g, n)
    cc = c_mat.reshape(bsz, nc, CHUNK, g, n)
    adt = (dt * a).reshape(bsz, nc, CHUNK, g, r).transpose(0, 3, 4, 1, 2)
    a_cs = jnp.cumsum(adt, axis=-1)
    decay_in = jnp.exp(segsum(adt))
    cb = jnp.einsum("bclgn,bcsgn->bgcls", cc, bc)
    y_diag = jnp.einsum("bgrcls,bcsgrp->bclgrp", cb[:, :, None] * decay_in, xd)
    decay_states = jnp.exp(a_cs[..., -1:] - a_cs).transpose(0, 3, 4, 1, 2)
    states = jnp.einsum("bclgn,bclgrp->bcgrpn", bc, xd * decay_states[..., None])
    init = jnp.zeros_like(states[:, :1])
    states = jnp.concatenate([init, states], axis=1)
    totals = jnp.pad(a_cs[..., -1], ((0, 0), (0, 0), (0, 0), (1, 0)))
    decay_chunk = jnp.exp(segsum(totals))
    new_states = jnp.einsum("bgrzc,bcgrpn->bzgrpn", decay_chunk, states)
    prev_states = new_states[:, :-1]
    decay_out = jnp.exp(a_cs).transpose(0, 3, 4, 1, 2)
    y_off = jnp.einsum("bclgn,bcgrpn->bclgrp", cc, prev_states) * decay_out[..., None]
    y = (y_diag + y_off).reshape(bsz, length, SSD_HEADS, p)
    return y + d_skip[:, None] * x


def hybrid_layer(x, p_i, norm_pre, norm_post, w_in, ssd_conv_w, ssd_conv_b, dt_bias,
                 a_log, d_skip, ssd_norm, sc_conv_w, w_out, w_ple_gate, w_ple_proj):
    bsz, length, _ = x.shape
    h = rmsnorm(x, norm_pre)
    proj = h @ w_in
    idx = list(np.cumsum(IN_SPLITS)[:-1])
    xbc, dt_raw, z_ssd, sc_h, sc_b, sc_c, z_sc = jnp.split(proj, idx, axis=-1)

    xbc = jax.nn.silu(causal_dwconv(xbc, ssd_conv_w, ssd_conv_b))
    xs, bm, cm = jnp.split(xbc, [D_MIX_SSD, D_MIX_SSD + SSD_GROUPS * D_STATE], axis=-1)
    xs = xs.reshape(bsz, length, SSD_HEADS, SSD_HEAD_DIM)
    bm = bm.reshape(bsz, length, SSD_GROUPS, D_STATE)
    cm = cm.reshape(bsz, length, SSD_GROUPS, D_STATE)
    dt = jax.nn.softplus(dt_raw + dt_bias)
    a = -jnp.exp(a_log)
    y_ssd = ssd_chunked(xs, dt, a, bm, cm, d_skip).reshape(bsz, length, D_MIX_SSD)
    yz = (y_ssd * jax.nn.silu(z_ssd)).reshape(bsz, length, SSD_GROUPS, D_MIX_SSD // SSD_GROUPS)
    y_ssd = rmsnorm(yz, ssd_norm.reshape(SSD_GROUPS, -1)).reshape(bsz, length, D_MIX_SSD)

    v = causal_dwconv(sc_c * sc_h, sc_conv_w)
    y_sc = sc_b * v * jax.nn.silu(z_sc)

    mix = jnp.concatenate([y_ssd, y_sc], axis=-1) @ w_out
    x = x + rmsnorm(mix, norm_post)

    gate = jax.nn.sigmoid(x @ w_ple_gate)
    return x + gate * (p_i @ w_ple_proj)


def setup_inputs(seed: int = 0) -> dict:
    key = jax.random.key(seed)
    ks = jax.random.split(key, 16)
    f32 = jnp.float32
    x = jax.random.normal(ks[0], (BATCH, SEQ, D_MODEL), f32)
    p = jax.random.normal(ks[1], (DEPTH, BATCH, SEQ, PLE_DIM), f32)
    norm_pre = 1.0 + 0.05 * jax.random.normal(ks[2], (DEPTH, D_MODEL), f32)
    norm_post = 1.0 + 0.05 * jax.random.normal(ks[3], (DEPTH, D_MODEL), f32)
    w_in = jax.random.normal(ks[4], (DEPTH, D_MODEL, IN_COLS), f32) * D_MODEL ** -0.5
    ssd_conv_w = jax.random.normal(ks[5], (DEPTH, SSD_CONV_W, SSD_CONV_DIM), f32) * SSD_CONV_W ** -0.5
    ssd_conv_b = 0.02 * jax.random.normal(ks[6], (DEPTH, SSD_CONV_DIM), f32)
    dt0 = jnp.exp(jax.random.uniform(ks[7], (DEPTH, SSD_HEADS), f32,
                                     math.log(1e-3), math.log(1e-1)))
    dt_bias = dt0 + jnp.log(-jnp.expm1(-dt0))
    a_log = jnp.log(jax.random.uniform(ks[8], (DEPTH, SSD_HEADS), f32, 1.0, 16.0))
    d_skip = 1.0 + 0.05 * jax.random.normal(ks[9], (DEPTH, SSD_HEADS), f32)
    ssd_norm = 1.0 + 0.05 * jax.random.normal(ks[10], (DEPTH, D_MIX_SSD), f32)
    sc_conv_w = jax.random.normal(ks[11], (DEPTH, SC_CONV_W, D_MIX_CONV), f32) * SC_CONV_W ** -0.5
    w_out = jax.random.normal(ks[12], (DEPTH, D_MIX, D_MODEL), f32) * D_MIX ** -0.5
    w_ple_gate = jax.random.normal(ks[13], (DEPTH, D_MODEL, D_MODEL), f32) * D_MODEL ** -0.5
    w_ple_proj = jax.random.normal(ks[14], (DEPTH, PLE_DIM, D_MODEL), f32) * (0.5 * PLE_DIM ** -0.5)
    return {"x": x, "p": p, "norm_pre": norm_pre, "norm_post": norm_post, "w_in": w_in,
            "ssd_conv_w": ssd_conv_w, "ssd_conv_b": ssd_conv_b, "dt_bias": dt_bias,
            "a_log": a_log, "d_skip": d_skip, "ssd_norm": ssd_norm, "sc_conv_w": sc_conv_w,
            "w_out": w_out, "w_ple_gate": w_ple_gate, "w_ple_proj": w_ple_proj}


def reference(x, p, norm_pre, norm_post, w_in, ssd_conv_w, ssd_conv_b, dt_bias, a_log,
              d_skip, ssd_norm, sc_conv_w, w_out, w_ple_gate, w_ple_proj):
    for i in range(DEPTH):
        x = hybrid_layer(x, p[i], norm_pre[i], norm_post[i], w_in[i], ssd_conv_w[i],
                         ssd_conv_b[i], dt_bias[i], a_log[i], d_skip[i], ssd_norm[i],
                         sc_conv_w[i], w_out[i], w_ple_gate[i], w_ple_proj[i])
    return x
```

```python
import functools

import numpy as np
import jax
import jax.numpy as jnp
from jax import lax
from jax.experimental import pallas as pl
from jax.experimental.pallas import tpu as pltpu

F32 = jnp.float32
BF16 = jnp.bfloat16

EPS = 1e-6
CHUNK = 128
HEAD_DIM = 64
N_HEADS = 16
N_GROUPS = 2
HEADS_PER_GROUP = N_HEADS // N_GROUPS
D_STATE = 64
D_SSD = N_HEADS * HEAD_DIM
GN = N_GROUPS * D_STATE
XBC = D_SSD + 2 * GN
SSD_CONV_W = 4
SC_CONV_W = 3
LANES = 128
SUBLANES = 8
DT_PAD = LANES
QUAD = 4
SC_COL_BLOCK = 256

OFF_XBC = 0
OFF_DT = OFF_XBC + XBC
OFF_Z = OFF_DT + DT_PAD


def _silu(v):
    return v * jax.nn.sigmoid(v)


def _softplus(v):
    return jnp.maximum(v, 0.0) + jnp.log1p(jnp.exp(-jnp.abs(v)))


def _mixer_kernel(x_ref, w_ref, npre_ref, cw_ref, cb_ref, dtb_ref, a_ref, dskip_ref,
                  snorm_ref, scw_ref, e2_ref, y_ref,
                  hb_scr, u_scr, xbc_scr, dt_scr, z_scr, q_scr, s_scr,
                  *, tl, d_mix_conv):
    n_chunks = tl // CHUNK
    off_sch = OFF_Z + D_SSD
    off_scb = off_sch + d_mix_conv
    off_scc = off_scb + d_mix_conv
    off_zsc = off_scc + d_mix_conv

    @pl.when(pl.program_id(1) == 0)
    def _():
        u_scr[0:SUBLANES, :] = jnp.zeros((SUBLANES, XBC), F32)
        q_scr[0:SUBLANES, :] = jnp.zeros((SUBLANES, d_mix_conv), F32)
        s_scr[...] = jnp.zeros_like(s_scr)

    x = x_ref[...]
    ms = jnp.mean(x * x, axis=-1, keepdims=True)
    hb_scr[...] = (x * lax.rsqrt(ms + EPS) * npre_ref[...]).astype(BF16)

    u_scr[SUBLANES:SUBLANES + tl, :] = jnp.dot(
        hb_scr[...], w_ref[:, OFF_XBC:OFF_XBC + XBC], preferred_element_type=F32)
    acc = cb_ref[...]
    for k in range(SSD_CONV_W):
        r0 = SUBLANES - (SSD_CONV_W - 1) + k
        acc = acc + cw_ref[k:k + 1, :] * u_scr[r0:r0 + tl, :]
    xbc_scr[...] = _silu(acc)
    u_scr[0:SUBLANES, :] = u_scr[tl:tl + SUBLANES, :]

    dt_raw = jnp.dot(hb_scr[...], w_ref[:, OFF_DT:OFF_DT + DT_PAD], preferred_element_type=F32)
    dt_scr[...] = _softplus(dt_raw + dtb_ref[...])
    z_scr[...] = jnp.dot(hb_scr[...], w_ref[:, OFF_Z:OFF_Z + D_SSD], preferred_element_type=F32)

    for cblk in range(d_mix_conv // SC_COL_BLOCK):
        c0 = cblk * SC_COL_BLOCK
        cs = slice(c0, c0 + SC_COL_BLOCK)

        def proj(off):
            return jnp.dot(hb_scr[...], w_ref[:, off + c0:off + c0 + SC_COL_BLOCK],
                           preferred_element_type=F32)

        q = proj(off_scc) * proj(off_sch)
        q_scr[SUBLANES:SUBLANES + tl, cs] = q
        v = scw_ref[SC_CONV_W - 1:SC_CONV_W, cs] * q
        for k in range(SC_CONV_W - 1):
            r0 = SUBLANES - (SC_CONV_W - 1) + k
            v = v + scw_ref[k:k + 1, cs] * q_scr[r0:r0 + tl, cs]
        y_sc = proj(off_scb) * v * _silu(proj(off_zsc))
        y_ref[:, D_SSD + c0:D_SSD + c0 + SC_COL_BLOCK] = y_sc.astype(BF16)
        q_scr[0:SUBLANES, cs] = q_scr[tl:tl + SUBLANES, cs]

    row_i = lax.broadcasted_iota(jnp.int32, (CHUNK, CHUNK), 0)
    col_i = lax.broadcasted_iota(jnp.int32, (CHUNK, CHUNK), 1)
    tril = row_i >= col_i
    tri_f = tril.astype(F32)
    lane_q = lax.broadcasted_iota(jnp.int32, (1, QUAD * HEAD_DIM), 1) // HEAD_DIM
    quad_masks = [(lane_q == j).astype(BF16) for j in range(QUAD)]
    s_row = lax.broadcasted_iota(jnp.int32, (GN, D_SSD), 0) // D_STATE
    s_col = lax.broadcasted_iota(jnp.int32, (GN, D_SSD), 1) // (HEADS_PER_GROUP * HEAD_DIM)
    state_mask = s_row == s_col

    for c in range(n_chunks):
        rs = slice(c * CHUNK, (c + 1) * CHUNK)
        xs = xbc_scr[rs, 0:D_SSD]
        bm = xbc_scr[rs, D_SSD:D_SSD + GN]
        cm = xbc_scr[rs, D_SSD + GN:XBC]
        dt = dt_scr[rs, :]
        adt = dt * a_ref[...]
        acs = jnp.dot(tri_f, adt, precision=lax.Precision.HIGHEST, preferred_element_type=F32)
        acs_t = acs.T
        dt_t = dt.T
        total = acs[CHUNK - 1:CHUNK, :]
        decay_states = jnp.exp(total - acs)
        decay_out = jnp.exp(acs)
        both = jnp.concatenate([decay_states * dt, decay_out], axis=0)
        hi = both.astype(BF16)
        lo = (both - hi.astype(F32)).astype(BF16)
        expanded = jnp.dot(jnp.concatenate([hi, lo], axis=1), e2_ref[...],
                           preferred_element_type=F32)
        wgt_e = expanded[0:CHUNK]
        dout_e = expanded[CHUNK:2 * CHUNK]

        bm_b = bm.astype(BF16)
        cm_b = cm.astype(BF16)
        xs_b = xs.astype(BF16)
        cbs = []
        for g in range(N_GROUPS):
            cm_g = jnp.where(col_i // D_STATE == g, cm, 0.0).astype(BF16)
            cbs.append(lax.dot_general(cm_g, bm_b, (((1,), (1,)), ((), ())),
                                       preferred_element_type=F32))

        y_parts = []
        for qd in range(N_HEADS // QUAD):
            ms_q = []
            for j in range(QUAD):
                h = qd * QUAD + j
                seg = acs[:, h:h + 1] - acs_t[h:h + 1, :]
                m = jnp.exp(jnp.where(tril, seg, -jnp.inf)) * cbs[h // HEADS_PER_GROUP]
                ms_q.append((m * dt_t[h:h + 1, :]).astype(BF16))
            lhs = jnp.concatenate(ms_q, axis=1)
            xq = xs_b[:, qd * QUAD * HEAD_DIM:(qd + 1) * QUAD * HEAD_DIM]
            rhs = jnp.concatenate([xq * quad_masks[j] for j in range(QUAD)], axis=0)
            y_parts.append(jnp.dot(lhs, rhs, preferred_element_type=F32))
        y_diag = jnp.concatenate(y_parts, axis=1)

        s_old = s_scr[...]
        y_off = jnp.dot(cm_b, s_old.astype(BF16), preferred_element_type=F32)
        y = y_diag + dout_e * y_off + dskip_ref[...] * xs

        xw = (xs * wgt_e).astype(BF16)
        s_chunk = jnp.dot(bm.T.astype(BF16), xw, preferred_element_type=F32)
        s_scr[...] = s_old * dout_e[CHUNK - 1:CHUNK, :] + jnp.where(state_mask, s_chunk, 0.0)

        yz = y * _silu(z_scr[rs, :])
        gw = D_SSD // N_GROUPS
        for g in range(N_GROUPS):
            seg = yz[:, g * gw:(g + 1) * gw]
            msq = jnp.mean(seg * seg, axis=-1, keepdims=True)
            y_ref[rs, g * gw:(g + 1) * gw] = (
                seg * lax.rsqrt(msq + EPS) * snorm_ref[:, g * gw:(g + 1) * gw]).astype(BF16)


def _merge_kernel(y_ref, x_ref, p_ref, wout_ref, npost_ref, wg_ref, wp_ref, o_ref):
    mix = jnp.dot(y_ref[...], wout_ref[...], preferred_element_type=F32)
    ms = jnp.mean(mix * mix, axis=-1, keepdims=True)
    x1 = x_ref[...] + mix * lax.rsqrt(ms + EPS) * npost_ref[...]
    gate = jax.nn.sigmoid(jnp.dot(x1.astype(BF16), wg_ref[...], preferred_element_type=F32))
    pp = jnp.dot(p_ref[...].astype(BF16), wp_ref[...], preferred_element_type=F32)
    o_ref[...] = x1 + gate * pp


def _resident(shape):
    return pl.BlockSpec(shape, lambda b, t: (0,) * len(shape), pipeline_mode=pl.Buffered(1))


def _expand_matrix():
    e = np.zeros((DT_PAD, D_SSD), np.float32)
    for h in range(N_HEADS):
        e[h, h * HEAD_DIM:(h + 1) * HEAD_DIM] = 1.0
    return jnp.asarray(np.concatenate([e, e], axis=0), BF16)


MIXER_TILE = 512
MERGE_TILE = 512
VMEM_LIMIT = 56 * 1024 * 1024


def _layer(x, p_i, norm_pre, norm_post, w_in, ssd_conv_w, ssd_conv_b, dt_bias, a_log, d_skip,
           ssd_norm, sc_conv_w, w_out, w_ple_gate, w_ple_proj):
    bsz, length, d_model = x.shape
    d_mix_conv = sc_conv_w.shape[-1]
    d_mix = D_SSD + d_mix_conv
    ple_dim = p_i.shape[-1]
    tl = min(MIXER_TILE, length)
    assert length % tl == 0 and tl % CHUNK == 0 and d_mix_conv % SC_COL_BLOCK == 0

    n_dt = OFF_DT + N_HEADS
    w_r = jnp.concatenate(
        [w_in[:, :n_dt], jnp.zeros((d_model, DT_PAD - N_HEADS), w_in.dtype), w_in[:, n_dt:]],
        axis=1).astype(BF16)
    pad = (0, DT_PAD - N_HEADS)
    dtb = jnp.pad(dt_bias, pad).reshape(1, DT_PAD)
    a_row = jnp.pad(-jnp.exp(a_log), pad).reshape(1, DT_PAD)
    dskip_row = jnp.repeat(d_skip, HEAD_DIM).reshape(1, D_SSD)
    in_cols = w_r.shape[1]

    mixer = pl.pallas_call(
        functools.partial(_mixer_kernel, tl=tl, d_mix_conv=d_mix_conv),
        grid=(bsz, length // tl),
        in_specs=[
            pl.BlockSpec((None, tl, d_model), lambda b, t: (b, t, 0)),
            _resident((d_model, in_cols)),
            _resident((1, d_model)),
            _resident((SSD_CONV_W, XBC)),
            _resident((1, XBC)),
            _resident((1, DT_PAD)),
            _resident((1, DT_PAD)),
            _resident((1, D_SSD)),
            _resident((1, D_SSD)),
            _resident((SC_CONV_W, d_mix_conv)),
            _resident((2 * DT_PAD, D_SSD)),
        ],
        out_specs=pl.BlockSpec((None, tl, d_mix), lambda b, t: (b, t, 0)),
        out_shape=jax.ShapeDtypeStruct((bsz, length, d_mix), BF16),
        scratch_shapes=[
            pltpu.VMEM((tl, d_model), BF16),
            pltpu.VMEM((tl + SUBLANES, XBC), F32),
            pltpu.VMEM((tl, XBC), F32),
            pltpu.VMEM((tl, DT_PAD), F32),
            pltpu.VMEM((tl, D_SSD), F32),
            pltpu.VMEM((tl + SUBLANES, d_mix_conv), F32),
            pltpu.VMEM((GN, D_SSD), F32),
        ],
        compiler_params=pltpu.CompilerParams(
            dimension_semantics=("arbitrary", "arbitrary"), vmem_limit_bytes=VMEM_LIMIT),
        name="mixer",
    )
    y = mixer(x, w_r, norm_pre.reshape(1, d_model), ssd_conv_w, ssd_conv_b.reshape(1, XBC),
              dtb, a_row, dskip_row, ssd_norm.reshape(1, D_SSD), sc_conv_w, _expand_matrix())

    tm = min(MERGE_TILE, length)
    assert length % tm == 0
    merge = pl.pallas_call(
        _merge_kernel,
        grid=(bsz, length // tm),
        in_specs=[
            pl.BlockSpec((None, tm, d_mix), lambda b, t: (b, t, 0)),
            pl.BlockSpec((None, tm, d_model), lambda b, t: (b, t, 0)),
            pl.BlockSpec((None, tm, ple_dim), lambda b, t: (b, t, 0)),
            _resident((d_mix, d_model)),
            _resident((1, d_model)),
            _resident((d_model, d_model)),
            _resident((ple_dim, d_model)),
        ],
        out_specs=pl.BlockSpec((None, tm, d_model), lambda b, t: (b, t, 0)),
        out_shape=jax.ShapeDtypeStruct((bsz, length, d_model), F32),
        compiler_params=pltpu.CompilerParams(
            dimension_semantics=("arbitrary", "arbitrary"), vmem_limit_bytes=VMEM_LIMIT),
        name="merge",
    )
    return merge(y, x, p_i, w_out.astype(BF16), norm_post.reshape(1, d_model),
                 w_ple_gate.astype(BF16), w_ple_proj.astype(BF16))


def kernel(x, p, norm_pre, norm_post, w_in, ssd_conv_w, ssd_conv_b, dt_bias, a_log, d_skip,
           ssd_norm, sc_conv_w, w_out, w_ple_gate, w_ple_proj):
    for i in range(p.shape[0]):
        x = _layer(x, p[i], norm_pre[i], norm_post[i], w_in[i], ssd_conv_w[i], ssd_conv_b[i],
                   dt_bias[i], a_log[i], d_skip[i], ssd_norm[i], sc_conv_w[i], w_out[i],
                   w_ple_gate[i], w_ple_proj[i])
    return x
```

```python
import functools

import numpy as np
import jax
import jax.numpy as jnp
from jax import lax
from jax.experimental import pallas as pl
from jax.experimental.pallas import tpu as pltpu

F32 = jnp.float32
BF16 = jnp.bfloat16

EPS = 1e-6
CHUNK = 128
HEAD_DIM = 64
N_HEADS = 16
N_GROUPS = 2
HEADS_PER_GROUP = N_HEADS // N_GROUPS
D_STATE = 64
D_SSD = N_HEADS * HEAD_DIM
GN = N_GROUPS * D_STATE
XBC = D_SSD + 2 * GN
SSD_CONV_W = 4
SC_CONV_W = 3
LANES = 128
SUBLANES = 8
DT_PAD = LANES
QUAD = 4
SC_COL_BLOCK = 256

OFF_XBC = 0
OFF_DT = OFF_XBC + XBC
OFF_Z = OFF_DT + DT_PAD


def _silu(v):
    return v * jax.nn.sigmoid(v)


def _softplus(v):
    return jnp.maximum(v, 0.0) + jnp.log1p(jnp.exp(-jnp.abs(v)))


def _mixer_kernel(x_ref, w_ref, npre_ref, cw_ref, cb_ref, dtb_ref, a_ref, dskip_ref,
                  snorm_ref, scw_ref, e2_ref, y_ref,
                  hb_scr, u_scr, xbc_scr, dt_scr, z_scr, q_scr, s_scr,
                  *, tl, d_mix_conv):
    n_chunks = tl // CHUNK
    off_sch = OFF_Z + D_SSD
    off_scb = off_sch + d_mix_conv
    off_scc = off_scb + d_mix_conv
    off_zsc = off_scc + d_mix_conv

    @pl.when(pl.program_id(1) == 0)
    def _():
        u_scr[0:SUBLANES, :] = jnp.zeros((SUBLANES, XBC), F32)
        q_scr[0:SUBLANES, :] = jnp.zeros((SUBLANES, d_mix_conv), F32)
        s_scr[...] = jnp.zeros_like(s_scr)

    x = x_ref[...]
    ms = jnp.mean(x * x, axis=-1, keepdims=True)
    hb_scr[...] = (x * lax.rsqrt(ms + EPS) * npre_ref[...]).astype(BF16)

    u_scr[SUBLANES:SUBLANES + tl, :] = jnp.dot(
        hb_scr[...], w_ref[:, OFF_XBC:OFF_XBC + XBC], preferred_element_type=F32)
    acc = cb_ref[...]
    for k in range(SSD_CONV_W):
        r0 = SUBLANES - (SSD_CONV_W - 1) + k
        acc = acc + cw_ref[k:k + 1, :] * u_scr[r0:r0 + tl, :]
    xbc_scr[...] = _silu(acc)
    u_scr[0:SUBLANES, :] = u_scr[tl:tl + SUBLANES, :]

    dt_raw = jnp.dot(hb_scr[...], w_ref[:, OFF_DT:OFF_DT + DT_PAD], preferred_element_type=F32)
    dt_scr[...] = _softplus(dt_raw + dtb_ref[...])
    z_scr[...] = jnp.dot(hb_scr[...], w_ref[:, OFF_Z:OFF_Z + D_SSD], preferred_element_type=F32)

    for cblk in range(d_mix_conv // SC_COL_BLOCK):
        c0 = cblk * SC_COL_BLOCK
        cs = slice(c0, c0 + SC_COL_BLOCK)

        def proj(off):
            return jnp.dot(hb_scr[...], w_ref[:, off + c0:off + c0 + SC_COL_BLOCK],
                           preferred_element_type=F32)

        q = proj(off_scc) * proj(off_sch)
        q_scr[SUBLANES:SUBLANES + tl, cs] = q
        v = scw_ref[SC_CONV_W - 1:SC_CONV_W, cs] * q
        for k in range(SC_CONV_W - 1):
            r0 = SUBLANES - (SC_CONV_W - 1) + k
            v = v + scw_ref[k:k + 1, cs] * q_scr[r0:r0 + tl, cs]
        y_sc = proj(off_scb) * v * _silu(proj(off_zsc))
        y_ref[:, D_SSD + c0:D_SSD + c0 + SC_COL_BLOCK] = y_sc.astype(BF16)
        q_scr[0:SUBLANES, cs] = q_scr[tl:tl + SUBLANES, cs]

    row_i = lax.broadcasted_iota(jnp.int32, (CHUNK, CHUNK), 0)
    col_i = lax.broadcasted_iota(jnp.int32, (CHUNK, CHUNK), 1)
    tril = row_i >= col_i
    tri_b = jnp.where(tril, 1.0, 0.0).astype(BF16)
    tri3 = jnp.concatenate([tri_b, tri_b, tri_b], axis=1)
    lane_q = lax.broadcasted_iota(jnp.int32, (1, QUAD * HEAD_DIM), 1) // HEAD_DIM
    quad_masks = [(lane_q == j).astype(BF16) for j in range(QUAD)]
    s_row = lax.broadcasted_iota(jnp.int32, (GN, D_SSD), 0) // D_STATE
    s_col = lax.broadcasted_iota(jnp.int32, (GN, D_SSD), 1) // (HEADS_PER_GROUP * HEAD_DIM)
    state_mask = s_row == s_col

    for c in range(n_chunks):
        rs = slice(c * CHUNK, (c + 1) * CHUNK)
        xs = xbc_scr[rs, 0:D_SSD]
        bm = xbc_scr[rs, D_SSD:D_SSD + GN]
        cm = xbc_scr[rs, D_SSD + GN:XBC]
        dt = dt_scr[rs, :]
        adt = dt * a_ref[...]
        a1 = adt.astype(BF16)
        r1 = adt - a1.astype(F32)
        a2 = r1.astype(BF16)
        a3 = (r1 - a2.astype(F32)).astype(BF16)
        acs = jnp.dot(tri3, jnp.concatenate([a1, a2, a3], axis=0), preferred_element_type=F32)
        acs_t = acs.T
        dt_t = dt.T
        total = acs[CHUNK - 1:CHUNK, :]
        decay_states = jnp.exp(total - acs)
        decay_out = jnp.exp(acs)
        both = jnp.concatenate([decay_states * dt, decay_out], axis=0)
        hi = both.astype(BF16)
        lo = (both - hi.astype(F32)).astype(BF16)
        expanded = jnp.dot(jnp.concatenate([hi, lo], axis=1), e2_ref[...],
                           preferred_element_type=F32)
        wgt_e = expanded[0:CHUNK]
        dout_e = expanded[CHUNK:2 * CHUNK]

        bm_b = bm.astype(BF16)
        cm_b = cm.astype(BF16)
        xs_b = xs.astype(BF16)
        cbs = []
        for g in range(N_GROUPS):
            cm_g = jnp.where(col_i // D_STATE == g, cm, 0.0).astype(BF16)
            cbs.append(lax.dot_general(cm_g, bm_b, (((1,), (1,)), ((), ())),
                                       preferred_element_type=F32))

        y_parts = []
        for qd in range(N_HEADS // QUAD):
            ms_q = []
            for j in range(QUAD):
                h = qd * QUAD + j
                seg = acs[:, h:h + 1] - acs_t[h:h + 1, :]
                m = jnp.exp(jnp.where(tril, seg, -jnp.inf)) * cbs[h // HEADS_PER_GROUP]
                ms_q.append((m * dt_t[h:h + 1, :]).astype(BF16))
            lhs = jnp.concatenate(ms_q, axis=1)
            xq = xs_b[:, qd * QUAD * HEAD_DIM:(qd + 1) * QUAD * HEAD_DIM]
            rhs = jnp.concatenate([xq * quad_masks[j] for j in range(QUAD)], axis=0)
            y_parts.append(jnp.dot(lhs, rhs, preferred_element_type=F32))
        y_diag = jnp.concatenate(y_parts, axis=1)

        s_old = s_scr[...]
        y_off = jnp.dot(cm_b, s_old.astype(BF16), preferred_element_type=F32)
        y = y_diag + dout_e * y_off + dskip_ref[...] * xs

        xw = (xs * wgt_e).astype(BF16)
        s_chunk = jnp.dot(bm.T.astype(BF16), xw, preferred_element_type=F32)
        s_scr[...] = s_old * dout_e[CHUNK - 1:CHUNK, :] + jnp.where(state_mask, s_chunk, 0.0)

        yz = y * _silu(z_scr[rs, :])
        gw = D_SSD // N_GROUPS
        for g in range(N_GROUPS):
            seg = yz[:, g * gw:(g + 1) * gw]
            msq = jnp.mean(seg * seg, axis=-1, keepdims=True)
            y_ref[rs, g * gw:(g + 1) * gw] = (
                seg * lax.rsqrt(msq + EPS) * snorm_ref[:, g * gw:(g + 1) * gw]).astype(BF16)


def _merge_kernel(y_ref, x_ref, p_ref, wout_ref, npost_ref, wg_ref, wp_ref, o_ref):
    mix = jnp.dot(y_ref[...], wout_ref[...], preferred_element_type=F32)
    ms = jnp.mean(mix * mix, axis=-1, keepdims=True)
    x1 = x_ref[...] + mix * lax.rsqrt(ms + EPS) * npost_ref[...]
    gate = jax.nn.sigmoid(jnp.dot(x1.astype(BF16), wg_ref[...], preferred_element_type=F32))
    pp = jnp.dot(p_ref[...].astype(BF16), wp_ref[...], preferred_element_type=F32)
    o_ref[...] = x1 + gate * pp


def _layer_block(shape, layer):
    return pl.BlockSpec((None,) + tuple(shape), lambda b, t: (layer,) + (0,) * len(shape),
                        pipeline_mode=pl.Buffered(1))


def _layer_tile(shape, layer):
    return pl.BlockSpec((None, None) + tuple(shape), lambda b, t: (layer, b, t, 0))


def _expand_matrix():
    e = np.zeros((DT_PAD, D_SSD), np.float32)
    for h in range(N_HEADS):
        e[h, h * HEAD_DIM:(h + 1) * HEAD_DIM] = 1.0
    return jnp.asarray(np.concatenate([e, e], axis=0), BF16)


MIXER_TILE = 512
MERGE_TILE = 512
VMEM_LIMIT = 56 * 1024 * 1024


def kernel(x, p, norm_pre, norm_post, w_in, ssd_conv_w, ssd_conv_b, dt_bias, a_log, d_skip,
           ssd_norm, sc_conv_w, w_out, w_ple_gate, w_ple_proj):
    depth, bsz, length, ple_dim = p.shape
    d_model = x.shape[-1]
    d_mix_conv = sc_conv_w.shape[-1]
    d_mix = D_SSD + d_mix_conv
    tl = min(MIXER_TILE, length)
    tm = min(MERGE_TILE, length)
    assert length % tl == 0 and tl % CHUNK == 0 and length % tm == 0
    assert d_mix_conv % SC_COL_BLOCK == 0

    n_dt = OFF_DT + N_HEADS
    w_r = jnp.concatenate(
        [w_in[:, :, :n_dt], jnp.zeros((depth, d_model, DT_PAD - N_HEADS), w_in.dtype),
         w_in[:, :, n_dt:]], axis=2).astype(BF16)
    in_cols = w_r.shape[-1]
    pad = ((0, 0), (0, DT_PAD - N_HEADS))
    dtb = jnp.pad(dt_bias, pad)[:, None, :]
    a_row = jnp.pad(-jnp.exp(a_log), pad)[:, None, :]
    dskip_row = jnp.repeat(d_skip, HEAD_DIM, axis=1)[:, None, :]
    w_out_b = w_out.astype(BF16)
    w_gate_b = w_ple_gate.astype(BF16)
    w_proj_b = w_ple_proj.astype(BF16)
    e2 = _expand_matrix()
    params = pltpu.CompilerParams(dimension_semantics=("arbitrary", "arbitrary"),
                                  vmem_limit_bytes=VMEM_LIMIT)

    for layer in range(depth):
        mixer = pl.pallas_call(
            functools.partial(_mixer_kernel, tl=tl, d_mix_conv=d_mix_conv),
            grid=(bsz, length // tl),
            in_specs=[
                pl.BlockSpec((None, tl, d_model), lambda b, t: (b, t, 0)),
                _layer_block((d_model, in_cols), layer),
                _layer_block((1, d_model), layer),
                _layer_block((SSD_CONV_W, XBC), layer),
                _layer_block((1, XBC), layer),
                _layer_block((1, DT_PAD), layer),
                _layer_block((1, DT_PAD), layer),
                _layer_block((1, D_SSD), layer),
                _layer_block((1, D_SSD), layer),
                _layer_block((SC_CONV_W, d_mix_conv), layer),
                pl.BlockSpec((2 * DT_PAD, D_SSD), lambda b, t: (0, 0),
                             pipeline_mode=pl.Buffered(1)),
            ],
            out_specs=pl.BlockSpec((None, tl, d_mix), lambda b, t: (b, t, 0)),
            out_shape=jax.ShapeDtypeStruct((bsz, length, d_mix), BF16),
            scratch_shapes=[
                pltpu.VMEM((tl, d_model), BF16),
                pltpu.VMEM((tl + SUBLANES, XBC), F32),
                pltpu.VMEM((tl, XBC), F32),
                pltpu.VMEM((tl, DT_PAD), F32),
                pltpu.VMEM((tl, D_SSD), F32),
                pltpu.VMEM((tl + SUBLANES, d_mix_conv), F32),
                pltpu.VMEM((GN, D_SSD), F32),
            ],
            compiler_params=params,
            name="mixer",
        )
        y = mixer(x, w_r, norm_pre[:, None, :], ssd_conv_w, ssd_conv_b[:, None, :], dtb, a_row,
                  dskip_row, ssd_norm[:, None, :], sc_conv_w, e2)

        merge = pl.pallas_call(
            _merge_kernel,
            grid=(bsz, length // tm),
            in_specs=[
                pl.BlockSpec((None, tm, d_mix), lambda b, t: (b, t, 0)),
                pl.BlockSpec((None, tm, d_model), lambda b, t: (b, t, 0)),
                _layer_tile((tm, ple_dim), layer),
                _layer_block((d_mix, d_model), layer),
                _layer_block((1, d_model), layer),
                _layer_block((d_model, d_model), layer),
                _layer_block((ple_dim, d_model), layer),
            ],
            out_specs=pl.BlockSpec((None, tm, d_model), lambda b, t: (b, t, 0)),
            out_shape=jax.ShapeDtypeStruct((bsz, length, d_model), F32),
            compiler_params=params,
            name="merge",
        )
        x = merge(y, x, p, w_out_b, norm_post[:, None, :], w_gate_b, w_proj_b)
    return x
```

```python
import functools

import numpy as np
import jax
import jax.numpy as jnp
from jax import lax
from jax.experimental import pallas as pl
from jax.experimental.pallas import tpu as pltpu

F32 = jnp.float32
BF16 = jnp.bfloat16

EPS = 1e-6
CHUNK = 128
HEAD_DIM = 64
N_HEADS = 16
N_GROUPS = 2
HEADS_PER_GROUP = N_HEADS // N_GROUPS
D_STATE = 64
D_SSD = N_HEADS * HEAD_DIM
GN = N_GROUPS * D_STATE
XBC = D_SSD + 2 * GN
SSD_CONV_W = 4
SC_CONV_W = 3
LANES = 128
SUBLANES = 8
DT_PAD = LANES
QUAD = 4
SC_COL_BLOCK = 256

OFF_XBC = 0
OFF_DT = OFF_XBC + XBC
HEAD_COLS = OFF_DT + DT_PAD


def _silu(v):
    return v * jax.nn.sigmoid(v)


def _softplus(v):
    return jnp.maximum(v, 0.0) + jnp.log1p(jnp.exp(-jnp.abs(v)))


def _layer_kernel(x_ref, xp_ref, p_ref, wh_ref, w_ref, npre_ref, cw_ref, cb_ref, dtb_ref, a_ref,
                  dskip_ref, snorm_ref, scw_ref, e2_ref, wout_ref, npost_ref, wg_ref, wp_ref,
                  o_ref,
                  hb_scr, u_scr, xbc_scr, dt_scr, z_scr, q_scr, s_scr, y_scr,
                  *, tl, tiles_per_seq, d_mix_conv):
    n_chunks = tl // CHUNK
    off_sch = D_SSD
    off_scb = off_sch + d_mix_conv
    off_scc = off_scb + d_mix_conv
    off_zsc = off_scc + d_mix_conv
    step = pl.program_id(0)

    @pl.when(step == 0)
    def _():
        y_scr[...] = jnp.zeros_like(y_scr)

    @pl.when(step % tiles_per_seq == 0)
    def _():
        u_scr[0:SUBLANES, :] = jnp.zeros((SUBLANES, XBC), F32)
        q_scr[0:SUBLANES, :] = jnp.zeros((SUBLANES, d_mix_conv), F32)
        s_scr[...] = jnp.zeros_like(s_scr)

    mix = jnp.dot(y_scr[...], wout_ref[...], preferred_element_type=F32)

    x = x_ref[...]
    ms = jnp.mean(x * x, axis=-1, keepdims=True)
    hb_scr[...] = (x * lax.rsqrt(ms + EPS) * npre_ref[...]).astype(BF16)

    u_scr[SUBLANES:SUBLANES + tl, :] = jnp.dot(
        hb_scr[...], wh_ref[:, OFF_XBC:OFF_XBC + XBC], preferred_element_type=F32)

    msm = jnp.mean(mix * mix, axis=-1, keepdims=True)
    x1 = xp_ref[...] + mix * lax.rsqrt(msm + EPS) * npost_ref[...]
    gate = jax.nn.sigmoid(jnp.dot(x1.astype(BF16), wg_ref[...], preferred_element_type=F32))
    pp = jnp.dot(p_ref[...].astype(BF16), wp_ref[...], preferred_element_type=F32)
    o_ref[...] = x1 + gate * pp

    acc = cb_ref[...]
    for k in range(SSD_CONV_W):
        r0 = SUBLANES - (SSD_CONV_W - 1) + k
        acc = acc + cw_ref[k:k + 1, :] * u_scr[r0:r0 + tl, :]
    xbc_scr[...] = _silu(acc)
    u_scr[0:SUBLANES, :] = u_scr[tl:tl + SUBLANES, :]

    dt_raw = jnp.dot(hb_scr[...], wh_ref[:, OFF_DT:OFF_DT + DT_PAD], preferred_element_type=F32)
    dt_scr[...] = _softplus(dt_raw + dtb_ref[...])
    z_scr[...] = jnp.dot(hb_scr[...], w_ref[:, 0:D_SSD], preferred_element_type=F32)

    for cblk in range(d_mix_conv // SC_COL_BLOCK):
        c0 = cblk * SC_COL_BLOCK
        cs = slice(c0, c0 + SC_COL_BLOCK)

        def proj(off):
            return jnp.dot(hb_scr[...], w_ref[:, off + c0:off + c0 + SC_COL_BLOCK],
                           preferred_element_type=F32)

        q = proj(off_scc) * proj(off_sch)
        q_scr[SUBLANES:SUBLANES + tl, cs] = q
        v = scw_ref[SC_CONV_W - 1:SC_CONV_W, cs] * q
        for k in range(SC_CONV_W - 1):
            r0 = SUBLANES - (SC_CONV_W - 1) + k
            v = v + scw_ref[k:k + 1, cs] * q_scr[r0:r0 + tl, cs]
        y_sc = proj(off_scb) * v * _silu(proj(off_zsc))
        y_scr[:, D_SSD + c0:D_SSD + c0 + SC_COL_BLOCK] = y_sc.astype(BF16)
        q_scr[0:SUBLANES, cs] = q_scr[tl:tl + SUBLANES, cs]

    row_i = lax.broadcasted_iota(jnp.int32, (CHUNK, CHUNK), 0)
    col_i = lax.broadcasted_iota(jnp.int32, (CHUNK, CHUNK), 1)
    tril = row_i >= col_i
    tri_b = jnp.where(tril, 1.0, 0.0).astype(BF16)
    tri3 = jnp.concatenate([tri_b, tri_b, tri_b], axis=1)
    lane_q = lax.broadcasted_iota(jnp.int32, (1, QUAD * HEAD_DIM), 1) // HEAD_DIM
    quad_masks = [(lane_q == j).astype(BF16) for j in range(QUAD)]
    s_row = lax.broadcasted_iota(jnp.int32, (GN, D_SSD), 0) // D_STATE
    s_col = lax.broadcasted_iota(jnp.int32, (GN, D_SSD), 1) // (HEADS_PER_GROUP * HEAD_DIM)
    state_mask = s_row == s_col

    for c in range(n_chunks):
        rs = slice(c * CHUNK, (c + 1) * CHUNK)
        xs = xbc_scr[rs, 0:D_SSD]
        bm = xbc_scr[rs, D_SSD:D_SSD + GN]
        cm = xbc_scr[rs, D_SSD + GN:XBC]
        dt = dt_scr[rs, :]
        adt = dt * a_ref[...]
        a1 = adt.astype(BF16)
        r1 = adt - a1.astype(F32)
        a2 = r1.astype(BF16)
        a3 = (r1 - a2.astype(F32)).astype(BF16)
        acs = jnp.dot(tri3, jnp.concatenate([a1, a2, a3], axis=0), preferred_element_type=F32)
        acs_t = acs.T
        dt_t = dt.T
        total = acs[CHUNK - 1:CHUNK, :]
        decay_states = jnp.exp(total - acs)
        decay_out = jnp.exp(acs)
        both = jnp.concatenate([decay_states * dt, decay_out], axis=0)
        hi = both.astype(BF16)
        lo = (both - hi.astype(F32)).astype(BF16)
        expanded = jnp.dot(jnp.concatenate([hi, lo], axis=1), e2_ref[...],
                           preferred_element_type=F32)
        wgt_e = expanded[0:CHUNK]
        dout_e = expanded[CHUNK:2 * CHUNK]

        bm_b = bm.astype(BF16)
        cm_b = cm.astype(BF16)
        xs_b = xs.astype(BF16)
        cbs = []
        for g in range(N_GROUPS):
            cm_g = jnp.where(col_i // D_STATE == g, cm, 0.0).astype(BF16)
            cbs.append(lax.dot_general(cm_g, bm_b, (((1,), (1,)), ((), ())),
                                       preferred_element_type=F32))

        y_parts = []
        for qd in range(N_HEADS // QUAD):
            ms_q = []
            for j in range(QUAD):
                h = qd * QUAD + j
                seg = acs[:, h:h + 1] - acs_t[h:h + 1, :]
                m = jnp.exp(jnp.where(tril, seg, -jnp.inf)) * cbs[h // HEADS_PER_GROUP]
                ms_q.append((m * dt_t[h:h + 1, :]).astype(BF16))
            lhs = jnp.concatenate(ms_q, axis=1)
            xq = xs_b[:, qd * QUAD * HEAD_DIM:(qd + 1) * QUAD * HEAD_DIM]
            rhs = jnp.concatenate([xq * quad_masks[j] for j in range(QUAD)], axis=0)
            y_parts.append(jnp.dot(lhs, rhs, preferred_element_type=F32))
        y_diag = jnp.concatenate(y_parts, axis=1)

        s_old = s_scr[...]
        y_off = jnp.dot(cm_b, s_old.astype(BF16), preferred_element_type=F32)
        y = y_diag + dout_e * y_off + dskip_ref[...] * xs

        xw = (xs * wgt_e).astype(BF16)
        s_chunk = jnp.dot(bm.T.astype(BF16), xw, preferred_element_type=F32)
        s_scr[...] = s_old * dout_e[CHUNK - 1:CHUNK, :] + jnp.where(state_mask, s_chunk, 0.0)

        yz = y * _silu(z_scr[rs, :])
        gw = D_SSD // N_GROUPS
        for g in range(N_GROUPS):
            seg = yz[:, g * gw:(g + 1) * gw]
            msq = jnp.mean(seg * seg, axis=-1, keepdims=True)
            y_scr[rs, g * gw:(g + 1) * gw] = (
                seg * lax.rsqrt(msq + EPS) * snorm_ref[:, g * gw:(g + 1) * gw]).astype(BF16)


def _layer_block(shape, layer):
    return pl.BlockSpec((None,) + tuple(shape), lambda s: (layer,) + (0,) * len(shape),
                        pipeline_mode=pl.Buffered(1))


def _expand_matrix():
    e = np.zeros((DT_PAD, D_SSD), np.float32)
    for h in range(N_HEADS):
        e[h, h * HEAD_DIM:(h + 1) * HEAD_DIM] = 1.0
    return jnp.asarray(np.concatenate([e, e], axis=0), BF16)


SEQ_TILE = 512
VMEM_LIMIT = 58 * 1024 * 1024


def kernel(x, p, norm_pre, norm_post, w_in, ssd_conv_w, ssd_conv_b, dt_bias, a_log, d_skip,
           ssd_norm, sc_conv_w, w_out, w_ple_gate, w_ple_proj):
    depth, bsz, length, ple_dim = p.shape
    d_model = x.shape[-1]
    d_mix_conv = sc_conv_w.shape[-1]
    d_mix = D_SSD + d_mix_conv
    tl = min(SEQ_TILE, length)
    assert length % tl == 0 and tl % CHUNK == 0 and d_mix_conv % SC_COL_BLOCK == 0
    tiles_per_seq = length // tl
    n_tiles = bsz * tiles_per_seq

    n_dt = OFF_DT + N_HEADS
    w_head = jnp.pad(w_in[:, :, :n_dt], ((0, 0), (0, 0), (0, DT_PAD - N_HEADS))).astype(BF16)
    w_main = w_in[:, :, n_dt:].astype(BF16)
    main_cols = w_main.shape[-1]
    pad = ((0, 0), (0, DT_PAD - N_HEADS))
    dtb = jnp.pad(dt_bias, pad)[:, None, :]
    a_row = jnp.pad(-jnp.exp(a_log), pad)[:, None, :]
    dskip_row = jnp.repeat(d_skip, HEAD_DIM, axis=1)[:, None, :]
    w_out_b = w_out.astype(BF16)
    w_gate_b = w_ple_gate.astype(BF16)
    w_proj_b = w_ple_proj.astype(BF16)
    e2 = _expand_matrix()

    def cur_tile(s):
        n = jnp.minimum(s, n_tiles - 1)
        return n // tiles_per_seq, n % tiles_per_seq

    def prev_tile(s):
        n = jnp.maximum(s - 1, 0)
        return n // tiles_per_seq, n % tiles_per_seq

    def ple_tile(layer, s):
        return (layer,) + prev_tile(s) + (0,)

    for layer in range(depth):
        call = pl.pallas_call(
            functools.partial(_layer_kernel, tl=tl, tiles_per_seq=tiles_per_seq,
                              d_mix_conv=d_mix_conv),
            grid=(n_tiles + 1,),
            in_specs=[
                pl.BlockSpec((None, tl, d_model), lambda s: cur_tile(s) + (0,)),
                pl.BlockSpec((None, tl, d_model), lambda s: prev_tile(s) + (0,)),
                pl.BlockSpec((None, None, tl, ple_dim), functools.partial(ple_tile, layer)),
                _layer_block((d_model, HEAD_COLS), layer),
                _layer_block((d_model, main_cols), layer),
                _layer_block((1, d_model), layer),
                _layer_block((SSD_CONV_W, XBC), layer),
                _layer_block((1, XBC), layer),
                _layer_block((1, DT_PAD), layer),
                _layer_block((1, DT_PAD), layer),
                _layer_block((1, D_SSD), layer),
                _layer_block((1, D_SSD), layer),
                _layer_block((SC_CONV_W, d_mix_conv), layer),
                pl.BlockSpec((2 * DT_PAD, D_SSD), lambda s: (0, 0), pipeline_mode=pl.Buffered(1)),
                _layer_block((d_mix, d_model), layer),
                _layer_block((1, d_model), layer),
                _layer_block((d_model, d_model), layer),
                _layer_block((ple_dim, d_model), layer),
            ],
            out_specs=pl.BlockSpec((None, tl, d_model), lambda s: prev_tile(s) + (0,)),
            out_shape=jax.ShapeDtypeStruct((bsz, length, d_model), F32),
            scratch_shapes=[
                pltpu.VMEM((tl, d_model), BF16),
                pltpu.VMEM((tl + SUBLANES, XBC), F32),
                pltpu.VMEM((tl, XBC), F32),
                pltpu.VMEM((tl, DT_PAD), F32),
                pltpu.VMEM((tl, D_SSD), F32),
                pltpu.VMEM((tl + SUBLANES, d_mix_conv), F32),
                pltpu.VMEM((GN, D_SSD), F32),
                pltpu.VMEM((tl, d_mix), BF16),
            ],
            compiler_params=pltpu.CompilerParams(dimension_semantics=("arbitrary",),
                                                 vmem_limit_bytes=VMEM_LIMIT),
            name="layer",
        )
        x = call(x, x, p, w_head, w_main, norm_pre[:, None, :], ssd_conv_w,
                 ssd_conv_b[:, None, :], dtb, a_row, dskip_row, ssd_norm[:, None, :], sc_conv_w,
                 e2, w_out_b, norm_post[:, None, :], w_gate_b, w_proj_b)
    return x
```

```python
import functools

import numpy as np
import jax
import jax.numpy as jnp
from jax import lax
from jax.experimental import pallas as pl
from jax.experimental.pallas import tpu as pltpu

F32 = jnp.float32
BF16 = jnp.bfloat16

EPS = 1e-6
CHUNK = 128
HEAD_DIM = 64
N_HEADS = 16
N_GROUPS = 2
HEADS_PER_GROUP = N_HEADS // N_GROUPS
D_STATE = 64
D_SSD = N_HEADS * HEAD_DIM
GN = N_GROUPS * D_STATE
XBC = D_SSD + 2 * GN
SSD_CONV_W = 4
SC_CONV_W = 3
LANES = 128
SUBLANES = 8
DT_PAD = LANES
QUAD = 4
SC_COL_BLOCK = 256

OFF_XBC = 0
OFF_DT = OFF_XBC + XBC
HEAD_COLS = OFF_DT + DT_PAD


def _silu(v):
    return v * jax.nn.sigmoid(v)


def _softplus(v):
    return jnp.maximum(v, 0.0) + jnp.log1p(jnp.exp(-jnp.abs(v)))


def _layer_kernel(x_ref, xp_ref, p_ref, wh_ref, w_ref, npre_ref, cw_ref, cb_ref, dtb_ref, a_ref,
                  dskip_ref, snorm_ref, scw_ref, e2_ref, wout_ref, npost_ref, wg_ref, wp_ref,
                  o_ref,
                  hb_scr, u_scr, xbc_scr, dt_scr, z_scr, q_scr, s_scr, y_scr, x1b_scr,
                  *, tl, tiles_per_seq, d_mix_conv):
    n_chunks = tl // CHUNK
    off_sch = D_SSD
    off_scb = off_sch + d_mix_conv
    off_scc = off_scb + d_mix_conv
    off_zsc = off_scc + d_mix_conv
    step = pl.program_id(0)

    col_blocks = d_mix_conv // SC_COL_BLOCK
    cb_w = SC_COL_BLOCK
    n_out_blocks = o_ref.shape[-1] // cb_w

    def merge_out_proj(j, r):
        rows = slice(r * (tl // 2), (r + 1) * (tl // 2))
        cs = slice(j * cb_w, (j + 1) * cb_w)
        o_ref[rows, cs] = jnp.dot(y_scr[rows, :], wout_ref[:, cs], preferred_element_type=F32)

    def merge_norm():
        mix = o_ref[...]
        msm = jnp.mean(mix * mix, axis=-1, keepdims=True)
        x1 = xp_ref[...] + mix * lax.rsqrt(msm + EPS) * npost_ref[...]
        o_ref[...] = x1
        x1b_scr[...] = x1.astype(BF16)

    def merge_gate(j):
        cs = slice(j * cb_w, (j + 1) * cb_w)
        gate = jax.nn.sigmoid(jnp.dot(x1b_scr[...], wg_ref[:, cs], preferred_element_type=F32))
        pp = jnp.dot(p_ref[...].astype(BF16), wp_ref[:, cs], preferred_element_type=F32)
        o_ref[:, cs] = o_ref[:, cs] + gate * pp

    def pre_norm():
        x = x_ref[...]
        ms = jnp.mean(x * x, axis=-1, keepdims=True)
        hb_scr[...] = (x * lax.rsqrt(ms + EPS) * npre_ref[...]).astype(BF16)

    def xbc_conv(j):
        cs = slice(j * cb_w, (j + 1) * cb_w)
        u_scr[SUBLANES:SUBLANES + tl, cs] = jnp.dot(hb_scr[...], wh_ref[:, cs],
                                                    preferred_element_type=F32)
        acc = cb_ref[:, cs]
        for k in range(SSD_CONV_W):
            r0 = SUBLANES - (SSD_CONV_W - 1) + k
            acc = acc + cw_ref[k:k + 1, cs] * u_scr[r0:r0 + tl, cs]
        xbc_scr[:, cs] = _silu(acc)
        u_scr[0:SUBLANES, cs] = u_scr[tl:tl + SUBLANES, cs]

    def dt_proj():
        dt_raw = jnp.dot(hb_scr[...], wh_ref[:, OFF_DT:OFF_DT + DT_PAD],
                         preferred_element_type=F32)
        dt_scr[...] = _softplus(dt_raw + dtb_ref[...])

    def z_proj(j):
        cs = slice(j * cb_w, (j + 1) * cb_w)
        z_scr[:, cs] = jnp.dot(hb_scr[...], w_ref[:, cs], preferred_element_type=F32)

    def sc_piece(cblk, i):
        c0 = cblk * cb_w
        cs = slice(c0, c0 + cb_w)
        body = slice(SUBLANES, SUBLANES + tl)

        def proj(off):
            return jnp.dot(hb_scr[...], w_ref[:, off + c0:off + c0 + cb_w],
                           preferred_element_type=F32)

        if i == 0:
            q_scr[body, cs] = proj(off_scc)
        elif i == 1:
            q_scr[body, cs] = q_scr[body, cs] * proj(off_sch)
        elif i == 2:
            v = scw_ref[SC_CONV_W - 1:SC_CONV_W, cs] * q_scr[body, cs]
            for k in range(SC_CONV_W - 1):
                r0 = SUBLANES - (SC_CONV_W - 1) + k
                v = v + scw_ref[k:k + 1, cs] * q_scr[r0:r0 + tl, cs]
            history = q_scr[tl:tl + SUBLANES, cs]
            q_scr[body, cs] = proj(off_scb) * v
            q_scr[0:SUBLANES, cs] = history
        else:
            y_sc = q_scr[body, cs] * _silu(proj(off_zsc))
            y_scr[:, D_SSD + c0:D_SSD + c0 + cb_w] = y_sc.astype(BF16)

    row_i = lax.broadcasted_iota(jnp.int32, (CHUNK, CHUNK), 0)
    col_i = lax.broadcasted_iota(jnp.int32, (CHUNK, CHUNK), 1)
    tril = row_i >= col_i
    tri_b = jnp.where(tril, 1.0, 0.0).astype(BF16)
    tri3 = jnp.concatenate([tri_b, tri_b, tri_b], axis=1)
    lane_q = lax.broadcasted_iota(jnp.int32, (1, QUAD * HEAD_DIM), 1) // HEAD_DIM
    quad_masks = [(lane_q == j).astype(BF16) for j in range(QUAD)]
    s_row = lax.broadcasted_iota(jnp.int32, (GN, D_SSD), 0) // D_STATE
    s_col = lax.broadcasted_iota(jnp.int32, (GN, D_SSD), 1) // (HEADS_PER_GROUP * HEAD_DIM)
    state_mask = s_row == s_col
    ck = {}

    def chunk_prep(c):
        rs = slice(c * CHUNK, (c + 1) * CHUNK)
        xs = xbc_scr[rs, 0:D_SSD]
        bm = xbc_scr[rs, D_SSD:D_SSD + GN]
        cm = xbc_scr[rs, D_SSD + GN:XBC]
        dt = dt_scr[rs, :]
        adt = dt * a_ref[...]
        a1 = adt.astype(BF16)
        r1 = adt - a1.astype(F32)
        a2 = r1.astype(BF16)
        a3 = (r1 - a2.astype(F32)).astype(BF16)
        acs = jnp.dot(tri3, jnp.concatenate([a1, a2, a3], axis=0), preferred_element_type=F32)
        total = acs[CHUNK - 1:CHUNK, :]
        decay_states = jnp.exp(total - acs)
        decay_out = jnp.exp(acs)
        both = jnp.concatenate([decay_states * dt, decay_out], axis=0)
        hi = both.astype(BF16)
        lo = (both - hi.astype(F32)).astype(BF16)
        expanded = jnp.dot(jnp.concatenate([hi, lo], axis=1), e2_ref[...],
                           preferred_element_type=F32)
        bm_b = bm.astype(BF16)
        cbs = []
        for g in range(N_GROUPS):
            cm_g = jnp.where(col_i // D_STATE == g, cm, 0.0).astype(BF16)
            cbs.append(lax.dot_general(cm_g, bm_b, (((1,), (1,)), ((), ())),
                                       preferred_element_type=F32))
        ck.clear()
        ck.update(rs=rs, xs=xs, bm=bm, cm_b=cm.astype(BF16), xs_b=xs.astype(BF16), acs=acs,
                  acs_t=acs.T, dt_t=dt.T, wgt_e=expanded[0:CHUNK],
                  dout_e=expanded[CHUNK:2 * CHUNK], cbs=cbs, y_parts=[])

    def chunk_quad(qd):
        ms_q = []
        for j in range(QUAD):
            h = qd * QUAD + j
            seg = ck["acs"][:, h:h + 1] - ck["acs_t"][h:h + 1, :]
            m = jnp.exp(jnp.where(tril, seg, -jnp.inf)) * ck["cbs"][h // HEADS_PER_GROUP]
            ms_q.append((m * ck["dt_t"][h:h + 1, :]).astype(BF16))
        lhs = jnp.concatenate(ms_q, axis=1)
        xq = ck["xs_b"][:, qd * QUAD * HEAD_DIM:(qd + 1) * QUAD * HEAD_DIM]
        rhs = jnp.concatenate([xq * quad_masks[j] for j in range(QUAD)], axis=0)
        ck["y_parts"].append(jnp.dot(lhs, rhs, preferred_element_type=F32))

    def chunk_finish():
        rs, xs, dout_e = ck["rs"], ck["xs"], ck["dout_e"]
        y_diag = jnp.concatenate(ck["y_parts"], axis=1)
        s_old = s_scr[...]
        y_off = jnp.dot(ck["cm_b"], s_old.astype(BF16), preferred_element_type=F32)
        y = y_diag + dout_e * y_off + dskip_ref[...] * xs

        xw = (xs * ck["wgt_e"]).astype(BF16)
        s_chunk = jnp.dot(ck["bm"].T.astype(BF16), xw, preferred_element_type=F32)
        s_scr[...] = s_old * dout_e[CHUNK - 1:CHUNK, :] + jnp.where(state_mask, s_chunk, 0.0)

        yz = y * _silu(z_scr[rs, :])
        gw = D_SSD // N_GROUPS
        for g in range(N_GROUPS):
            seg = yz[:, g * gw:(g + 1) * gw]
            msq = jnp.mean(seg * seg, axis=-1, keepdims=True)
            y_scr[rs, g * gw:(g + 1) * gw] = (
                seg * lax.rsqrt(msq + EPS) * snorm_ref[:, g * gw:(g + 1) * gw]).astype(BF16)

    @pl.when(step == 0)
    def _():
        y_scr[...] = jnp.zeros_like(y_scr)

    @pl.when(step % tiles_per_seq == 0)
    def _():
        u_scr[0:SUBLANES, :] = jnp.zeros((SUBLANES, XBC), F32)
        q_scr[0:SUBLANES, :] = jnp.zeros((SUBLANES, d_mix_conv), F32)
        s_scr[...] = jnp.zeros_like(s_scr)

    fillers = [functools.partial(merge_out_proj, j, r)
               for j in range(n_out_blocks) for r in range(2)]
    fillers += [functools.partial(z_proj, j) for j in range(D_SSD // cb_w)]
    n_before_finish = len(fillers)
    fillers += [merge_norm]
    fillers += [functools.partial(merge_gate, j) for j in range(n_out_blocks)]
    fillers += [functools.partial(sc_piece, cblk, i) for cblk in range(col_blocks) for i in range(4)]
    chain = [pre_norm] + [functools.partial(xbc_conv, j) for j in range(XBC // cb_w)] + [dt_proj]
    for c in range(n_chunks):
        chain += [functools.partial(chunk_prep, c)]
        chain += [functools.partial(chunk_quad, qd) for qd in range(N_HEADS // QUAD)]
        chain += [chunk_finish]

    fillers.reverse()
    for _ in range(max(n_before_finish - chain.index(chunk_finish), 1)):
        fillers.pop()()
    for piece in chain:
        piece()
        if fillers:
            fillers.pop()()
    while fillers:
        fillers.pop()()


def _layer_block(shape, layer):
    return pl.BlockSpec((None,) + tuple(shape), lambda s: (layer,) + (0,) * len(shape),
                        pipeline_mode=pl.Buffered(1))


def _expand_matrix():
    e = np.zeros((DT_PAD, D_SSD), np.float32)
    for h in range(N_HEADS):
        e[h, h * HEAD_DIM:(h + 1) * HEAD_DIM] = 1.0
    return jnp.asarray(np.concatenate([e, e], axis=0), BF16)


SEQ_TILE = 512
VMEM_LIMIT = 58 * 1024 * 1024


def kernel(x, p, norm_pre, norm_post, w_in, ssd_conv_w, ssd_conv_b, dt_bias, a_log, d_skip,
           ssd_norm, sc_conv_w, w_out, w_ple_gate, w_ple_proj):
    depth, bsz, length, ple_dim = p.shape
    d_model = x.shape[-1]
    d_mix_conv = sc_conv_w.shape[-1]
    d_mix = D_SSD + d_mix_conv
    tl = min(SEQ_TILE, length)
    assert length % tl == 0 and tl % CHUNK == 0 and d_mix_conv % SC_COL_BLOCK == 0
    tiles_per_seq = length // tl
    n_tiles = bsz * tiles_per_seq

    n_dt = OFF_DT + N_HEADS
    w_head = jnp.pad(w_in[:, :, :n_dt], ((0, 0), (0, 0), (0, DT_PAD - N_HEADS))).astype(BF16)
    w_main = w_in[:, :, n_dt:].astype(BF16)
    main_cols = w_main.shape[-1]
    pad = ((0, 0), (0, DT_PAD - N_HEADS))
    dtb = jnp.pad(dt_bias, pad)[:, None, :]
    a_row = jnp.pad(-jnp.exp(a_log), pad)[:, None, :]
    dskip_row = jnp.repeat(d_skip, HEAD_DIM, axis=1)[:, None, :]
    w_out_b = w_out.astype(BF16)
    w_gate_b = w_ple_gate.astype(BF16)
    w_proj_b = w_ple_proj.astype(BF16)
    e2 = _expand_matrix()

    def cur_tile(s):
        n = jnp.minimum(s, n_tiles - 1)
        return n // tiles_per_seq, n % tiles_per_seq

    def prev_tile(s):
        n = jnp.maximum(s - 1, 0)
        return n // tiles_per_seq, n % tiles_per_seq

    def ple_tile(layer, s):
        return (layer,) + prev_tile(s) + (0,)

    for layer in range(depth):
        call = pl.pallas_call(
            functools.partial(_layer_kernel, tl=tl, tiles_per_seq=tiles_per_seq,
                              d_mix_conv=d_mix_conv),
            grid=(n_tiles + 1,),
            in_specs=[
                pl.BlockSpec((None, tl, d_model), lambda s: cur_tile(s) + (0,)),
                pl.BlockSpec((None, tl, d_model), lambda s: prev_tile(s) + (0,)),
                pl.BlockSpec((None, None, tl, ple_dim), functools.partial(ple_tile, layer)),
                _layer_block((d_model, HEAD_COLS), layer),
                _layer_block((d_model, main_cols), layer),
                _layer_block((1, d_model), layer),
                _layer_block((SSD_CONV_W, XBC), layer),
                _layer_block((1, XBC), layer),
                _layer_block((1, DT_PAD), layer),
                _layer_block((1, DT_PAD), layer),
                _layer_block((1, D_SSD), layer),
                _layer_block((1, D_SSD), layer),
                _layer_block((SC_CONV_W, d_mix_conv), layer),
                pl.BlockSpec((2 * DT_PAD, D_SSD), lambda s: (0, 0), pipeline_mode=pl.Buffered(1)),
                _layer_block((d_mix, d_model), layer),
                _layer_block((1, d_model), layer),
                _layer_block((d_model, d_model), layer),
                _layer_block((ple_dim, d_model), layer),
            ],
            out_specs=pl.BlockSpec((None, tl, d_model), lambda s: prev_tile(s) + (0,)),
            out_shape=jax.ShapeDtypeStruct((bsz, length, d_model), F32),
            scratch_shapes=[
                pltpu.VMEM((tl, d_model), BF16),
                pltpu.VMEM((tl + SUBLANES, XBC), F32),
                pltpu.VMEM((tl, XBC), F32),
                pltpu.VMEM((tl, DT_PAD), F32),
                pltpu.VMEM((tl, D_SSD), F32),
                pltpu.VMEM((tl + SUBLANES, d_mix_conv), F32),
                pltpu.VMEM((GN, D_SSD), F32),
                pltpu.VMEM((tl, d_mix), BF16),
                pltpu.VMEM((tl, d_model), BF16),
            ],
            compiler_params=pltpu.CompilerParams(
                dimension_semantics=("arbitrary",), vmem_limit_bytes=VMEM_LIMIT),
            name="layer",
        )
        x = call(x, x, p, w_head, w_main, norm_pre[:, None, :], ssd_conv_w,
                 ssd_conv_b[:, None, :], dtb, a_row, dskip_row, ssd_norm[:, None, :], sc_conv_w,
                 e2, w_out_b, norm_post[:, None, :], w_gate_b, w_proj_b)
    return x
```

```python
import functools

import numpy as np
import jax
import jax.numpy as jnp
from jax import lax
from jax.experimental import pallas as pl
from jax.experimental.pallas import tpu as pltpu

F32 = jnp.float32
BF16 = jnp.bfloat16

EPS = 1e-6
CHUNK = 128
HEAD_DIM = 64
N_HEADS = 16
N_GROUPS = 2
HEADS_PER_GROUP = N_HEADS // N_GROUPS
D_STATE = 64
D_SSD = N_HEADS * HEAD_DIM
GN = N_GROUPS * D_STATE
XBC = D_SSD + 2 * GN
SSD_CONV_W = 4
SC_CONV_W = 3
LANES = 128
SUBLANES = 8
DT_PAD = LANES
QUAD = 4
SC_COL_BLOCK = 256

OFF_XBC = 0
OFF_DT = OFF_XBC + XBC
HEAD_COLS = OFF_DT + DT_PAD


def _sigmoid(v):
    return 0.5 + 0.5 * jnp.tanh(0.5 * v)


def _silu(v):
    h = 0.5 * v
    return h + h * jnp.tanh(h)


def _causal_taps(hist_and_body, taps):
    n_taps = taps.shape[0]
    acc = taps[n_taps - 1:n_taps, :] * hist_and_body[SUBLANES:, :]
    for k in range(n_taps - 1):
        shifted = pltpu.roll(hist_and_body, n_taps - 1 - k, axis=0)
        acc = acc + taps[k:k + 1, :] * shifted[SUBLANES:, :]
    return acc


def _softplus(v):
    return jnp.maximum(v, 0.0) + jnp.log1p(jnp.exp(-jnp.abs(v)))


def _layer_kernel(x_ref, xp_ref, p_ref, wh_ref, w_ref, npre_ref, cw_ref, cb_ref, dtb_ref, a_ref,
                  dskip_ref, snorm_ref, scw_ref, e2_ref, wout_ref, npost_ref, wg_ref, wp_ref,
                  o_ref,
                  hb_scr, u_scr, xbc_scr, dt_scr, z_scr, q_scr, s_scr, y_scr, x1b_scr,
                  acs_scr, acst_scr, dtt_scr, exp_scr,
                  *, tl, tiles_per_seq, d_mix_conv):
    n_chunks = tl // CHUNK
    off_sch = D_SSD
    off_scb = off_sch + d_mix_conv
    off_scc = off_scb + d_mix_conv
    off_zsc = off_scc + d_mix_conv
    step = pl.program_id(0)

    col_blocks = d_mix_conv // SC_COL_BLOCK
    cb_w = SC_COL_BLOCK
    n_out_blocks = o_ref.shape[-1] // cb_w

    def merge_out_proj(j, r):
        rows = slice(r * (tl // 2), (r + 1) * (tl // 2))
        cs = slice(j * cb_w, (j + 1) * cb_w)
        o_ref[rows, cs] = jnp.dot(y_scr[rows, :], wout_ref[:, cs], preferred_element_type=F32)

    def merge_norm():
        mix = o_ref[...]
        msm = jnp.mean(mix * mix, axis=-1, keepdims=True)
        x1 = xp_ref[...] + mix * lax.rsqrt(msm + EPS) * npost_ref[...]
        o_ref[...] = x1
        x1b_scr[...] = x1.astype(BF16)

    def merge_gate(j):
        cs = slice(j * cb_w, (j + 1) * cb_w)
        gate = _sigmoid(jnp.dot(x1b_scr[...], wg_ref[:, cs], preferred_element_type=F32))
        pp = jnp.dot(p_ref[...].astype(BF16), wp_ref[:, cs], preferred_element_type=F32)
        o_ref[:, cs] = o_ref[:, cs] + gate * pp

    def pre_norm():
        x = x_ref[...]
        ms = jnp.mean(x * x, axis=-1, keepdims=True)
        hb_scr[...] = (x * lax.rsqrt(ms + EPS) * npre_ref[...]).astype(BF16)

    def xbc_conv(j):
        cs = slice(j * cb_w, (j + 1) * cb_w)
        u_scr[SUBLANES:SUBLANES + tl, cs] = jnp.dot(hb_scr[...], wh_ref[:, cs],
                                                    preferred_element_type=F32)
        xbc_scr[:, cs] = _silu(cb_ref[:, cs] + _causal_taps(u_scr[:, cs], cw_ref[:, cs]))
        u_scr[0:SUBLANES, cs] = u_scr[tl:tl + SUBLANES, cs]

    def dt_proj():
        dt_raw = jnp.dot(hb_scr[...], wh_ref[:, OFF_DT:OFF_DT + DT_PAD],
                         preferred_element_type=F32)
        dt_scr[...] = _softplus(dt_raw + dtb_ref[...])

    def z_proj(j):
        cs = slice(j * cb_w, (j + 1) * cb_w)
        z_scr[:, cs] = jnp.dot(hb_scr[...], w_ref[:, cs], preferred_element_type=F32)

    def sc_piece(cblk, i):
        c0 = cblk * cb_w
        cs = slice(c0, c0 + cb_w)
        body = slice(SUBLANES, SUBLANES + tl)

        def proj(off):
            return jnp.dot(hb_scr[...], w_ref[:, off + c0:off + c0 + cb_w],
                           preferred_element_type=F32)

        if i == 0:
            q_scr[body, cs] = proj(off_scc)
        elif i == 1:
            q_scr[body, cs] = q_scr[body, cs] * proj(off_sch)
        elif i == 2:
            v = _causal_taps(q_scr[:, cs], scw_ref[:, cs])
            history = q_scr[tl:tl + SUBLANES, cs]
            q_scr[body, cs] = proj(off_scb) * v
            q_scr[0:SUBLANES, cs] = history
        else:
            y_sc = q_scr[body, cs] * _silu(proj(off_zsc))
            y_scr[:, D_SSD + c0:D_SSD + c0 + cb_w] = y_sc.astype(BF16)

    row_i = lax.broadcasted_iota(jnp.int32, (CHUNK, CHUNK), 0)
    col_i = lax.broadcasted_iota(jnp.int32, (CHUNK, CHUNK), 1)
    tril = row_i >= col_i
    tri_b = jnp.where(tril, 1.0, 0.0).astype(BF16)
    tri3 = jnp.concatenate([tri_b, tri_b, tri_b], axis=1)
    lane_q = lax.broadcasted_iota(jnp.int32, (1, QUAD * HEAD_DIM), 1) // HEAD_DIM
    quad_masks = [(lane_q == j).astype(BF16) for j in range(QUAD)]
    s_row = lax.broadcasted_iota(jnp.int32, (GN, D_SSD), 0) // D_STATE
    s_col = lax.broadcasted_iota(jnp.int32, (GN, D_SSD), 1) // (HEADS_PER_GROUP * HEAD_DIM)
    state_mask = s_row == s_col
    ck = {}

    def tile_prep():
        dt = dt_scr[...]
        adt = dt * a_ref[...]
        adt_l = jnp.concatenate([adt[c * CHUNK:(c + 1) * CHUNK] for c in range(n_chunks)], axis=1)
        a1 = adt_l.astype(BF16)
        r1 = adt_l - a1.astype(F32)
        a2 = r1.astype(BF16)
        a3 = (r1 - a2.astype(F32)).astype(BF16)
        acs_l = jnp.dot(tri3, jnp.concatenate([a1, a2, a3], axis=0), preferred_element_type=F32)
        factors = []
        for c in range(n_chunks):
            rs = slice(c * CHUNK, (c + 1) * CHUNK)
            acs = acs_l[:, rs]
            total = acs[CHUNK - 1:CHUNK, :]
            factors += [jnp.exp(total - acs) * dt[rs], jnp.exp(acs)]
            acs_scr[rs, :] = acs
            acst_scr[rs, :] = acs.T
            dtt_scr[rs, :] = dt[rs].T
        both = jnp.concatenate(factors, axis=0)
        hi = both.astype(BF16)
        lo = (both - hi.astype(F32)).astype(BF16)
        exp_scr[...] = jnp.dot(jnp.concatenate([hi, lo], axis=1), e2_ref[...],
                               preferred_element_type=F32)

    def chunk_prep(c):
        rs = slice(c * CHUNK, (c + 1) * CHUNK)
        xs = xbc_scr[rs, 0:D_SSD]
        bm = xbc_scr[rs, D_SSD:D_SSD + GN]
        cm = xbc_scr[rs, D_SSD + GN:XBC]
        bm_b = bm.astype(BF16)
        cbs = []
        for g in range(N_GROUPS):
            cm_g = jnp.where(col_i // D_STATE == g, cm, 0.0).astype(BF16)
            cbs.append(lax.dot_general(cm_g, bm_b, (((1,), (1,)), ((), ())),
                                       preferred_element_type=F32))
        ck.clear()
        ck.update(c=c, rs=rs, xs=xs, bm=bm, cm_b=cm.astype(BF16), xs_b=xs.astype(BF16), cbs=cbs,
                  y_parts=[])

    def chunk_quad(qd):
        c, rs = ck["c"], ck["rs"]
        ms_q = []
        for j in range(QUAD):
            h = qd * QUAD + j
            row = slice(c * CHUNK + h, c * CHUNK + h + 1)
            seg = acs_scr[rs, h:h + 1] - acst_scr[row, :]
            m = jnp.exp(jnp.where(tril, seg, -jnp.inf)) * ck["cbs"][h // HEADS_PER_GROUP]
            ms_q.append((m * dtt_scr[row, :]).astype(BF16))
        lhs = jnp.concatenate(ms_q, axis=1)
        xq = ck["xs_b"][:, qd * QUAD * HEAD_DIM:(qd + 1) * QUAD * HEAD_DIM]
        rhs = jnp.concatenate([xq * quad_masks[j] for j in range(QUAD)], axis=0)
        ck["y_parts"].append(jnp.dot(lhs, rhs, preferred_element_type=F32))

    def chunk_finish():
        c, rs, xs = ck["c"], ck["rs"], ck["xs"]
        wgt_e = exp_scr[2 * c * CHUNK:(2 * c + 1) * CHUNK, :]
        dout_e = exp_scr[(2 * c + 1) * CHUNK:(2 * c + 2) * CHUNK, :]
        y_diag = jnp.concatenate(ck["y_parts"], axis=1)
        s_old = s_scr[...]
        y_off = jnp.dot(ck["cm_b"], s_old.astype(BF16), preferred_element_type=F32)
        y = y_diag + dout_e * y_off + dskip_ref[...] * xs

        xw = (xs * wgt_e).astype(BF16)
        s_chunk = jnp.dot(ck["bm"].T.astype(BF16), xw, preferred_element_type=F32)
        s_scr[...] = s_old * dout_e[CHUNK - 1:CHUNK, :] + jnp.where(state_mask, s_chunk, 0.0)

        yz = y * _silu(z_scr[rs, :])
        gw = D_SSD // N_GROUPS
        for g in range(N_GROUPS):
            seg = yz[:, g * gw:(g + 1) * gw]
            msq = jnp.mean(seg * seg, axis=-1, keepdims=True)
            y_scr[rs, g * gw:(g + 1) * gw] = (
                seg * lax.rsqrt(msq + EPS) * snorm_ref[:, g * gw:(g + 1) * gw]).astype(BF16)

    @pl.when(step == 0)
    def _():
        y_scr[...] = jnp.zeros_like(y_scr)

    @pl.when(step % tiles_per_seq == 0)
    def _():
        u_scr[0:SUBLANES, :] = jnp.zeros((SUBLANES, XBC), F32)
        q_scr[0:SUBLANES, :] = jnp.zeros((SUBLANES, d_mix_conv), F32)
        s_scr[...] = jnp.zeros_like(s_scr)

    fillers = [functools.partial(merge_out_proj, j, r)
               for j in range(n_out_blocks) for r in range(2)]
    fillers += [functools.partial(z_proj, j) for j in range(D_SSD // cb_w)]
    n_before_finish = len(fillers)
    fillers += [merge_norm]
    fillers += [functools.partial(merge_gate, j) for j in range(n_out_blocks)]
    fillers += [functools.partial(sc_piece, cblk, i) for cblk in range(col_blocks) for i in range(4)]
    chain = [pre_norm] + [functools.partial(xbc_conv, j) for j in range(XBC // cb_w)]
    chain += [dt_proj, tile_prep]
    for c in range(n_chunks):
        chain += [functools.partial(chunk_prep, c)]
        chain += [functools.partial(chunk_quad, qd) for qd in range(N_HEADS // QUAD)]
        chain += [chunk_finish]

    fillers.reverse()
    for _ in range(max(n_before_finish - chain.index(chunk_finish), 1)):
        fillers.pop()()
    for piece in chain:
        piece()
        if fillers:
            fillers.pop()()
    while fillers:
        fillers.pop()()


def _layer_block(shape, layer):
    return pl.BlockSpec((None,) + tuple(shape), lambda s: (layer,) + (0,) * len(shape),
                        pipeline_mode=pl.Buffered(1))


def _expand_matrix():
    e = np.zeros((DT_PAD, D_SSD), np.float32)
    for h in range(N_HEADS):
        e[h, h * HEAD_DIM:(h + 1) * HEAD_DIM] = 1.0
    return jnp.asarray(np.concatenate([e, e], axis=0), BF16)


SEQ_TILE = 512
VMEM_LIMIT = 58 * 1024 * 1024


def kernel(x, p, norm_pre, norm_post, w_in, ssd_conv_w, ssd_conv_b, dt_bias, a_log, d_skip,
           ssd_norm, sc_conv_w, w_out, w_ple_gate, w_ple_proj):
    depth, bsz, length, ple_dim = p.shape
    d_model = x.shape[-1]
    d_mix_conv = sc_conv_w.shape[-1]
    d_mix = D_SSD + d_mix_conv
    tl = min(SEQ_TILE, length)
    assert length % tl == 0 and tl % CHUNK == 0 and d_mix_conv % SC_COL_BLOCK == 0
    tiles_per_seq = length // tl
    n_tiles = bsz * tiles_per_seq

    n_dt = OFF_DT + N_HEADS
    w_head = jnp.pad(w_in[:, :, :n_dt], ((0, 0), (0, 0), (0, DT_PAD - N_HEADS))).astype(BF16)
    w_main = w_in[:, :, n_dt:].astype(BF16)
    main_cols = w_main.shape[-1]
    pad = ((0, 0), (0, DT_PAD - N_HEADS))
    dtb = jnp.pad(dt_bias, pad)[:, None, :]
    a_row = jnp.pad(-jnp.exp(a_log), pad)[:, None, :]
    dskip_row = jnp.repeat(d_skip, HEAD_DIM, axis=1)[:, None, :]
    w_out_b = w_out.astype(BF16)
    w_gate_b = w_ple_gate.astype(BF16)
    w_proj_b = w_ple_proj.astype(BF16)
    e2 = _expand_matrix()

    def cur_tile(s):
        n = jnp.minimum(s, n_tiles - 1)
        return n // tiles_per_seq, n % tiles_per_seq

    def prev_tile(s):
        n = jnp.maximum(s - 1, 0)
        return n // tiles_per_seq, n % tiles_per_seq

    def ple_tile(layer, s):
        return (layer,) + prev_tile(s) + (0,)

    for layer in range(depth):
        call = pl.pallas_call(
            functools.partial(_layer_kernel, tl=tl, tiles_per_seq=tiles_per_seq,
                              d_mix_conv=d_mix_conv),
            grid=(n_tiles + 1,),
            in_specs=[
                pl.BlockSpec((None, tl, d_model), lambda s: cur_tile(s) + (0,)),
                pl.BlockSpec((None, tl, d_model), lambda s: prev_tile(s) + (0,)),
                pl.BlockSpec((None, None, tl, ple_dim), functools.partial(ple_tile, layer)),
                _layer_block((d_model, HEAD_COLS), layer),
                _layer_block((d_model, main_cols), layer),
                _layer_block((1, d_model), layer),
                _layer_block((SSD_CONV_W, XBC), layer),
                _layer_block((1, XBC), layer),
                _layer_block((1, DT_PAD), layer),
                _layer_block((1, DT_PAD), layer),
                _layer_block((1, D_SSD), layer),
                _layer_block((1, D_SSD), layer),
                _layer_block((SC_CONV_W, d_mix_conv), layer),
                pl.BlockSpec((2 * DT_PAD, D_SSD), lambda s: (0, 0), pipeline_mode=pl.Buffered(1)),
                _layer_block((d_mix, d_model), layer),
                _layer_block((1, d_model), layer),
                _layer_block((d_model, d_model), layer),
                _layer_block((ple_dim, d_model), layer),
            ],
            out_specs=pl.BlockSpec((None, tl, d_model), lambda s: prev_tile(s) + (0,)),
            out_shape=jax.ShapeDtypeStruct((bsz, length, d_model), F32),
            scratch_shapes=[
                pltpu.VMEM((tl, d_model), BF16),
                pltpu.VMEM((tl + SUBLANES, XBC), F32),
                pltpu.VMEM((tl, XBC), F32),
                pltpu.VMEM((tl, DT_PAD), F32),
                pltpu.VMEM((tl, D_SSD), F32),
                pltpu.VMEM((tl + SUBLANES, d_mix_conv), F32),
                pltpu.VMEM((GN, D_SSD), F32),
                pltpu.VMEM((tl, d_mix), BF16),
                pltpu.VMEM((tl, d_model), BF16),
                pltpu.VMEM((tl, DT_PAD), F32),
                pltpu.VMEM((tl, CHUNK), F32),
                pltpu.VMEM((tl, CHUNK), F32),
                pltpu.VMEM((2 * tl, D_SSD), F32),
            ],
            compiler_params=pltpu.CompilerParams(
                dimension_semantics=("arbitrary",), vmem_limit_bytes=VMEM_LIMIT),
            name="layer",
        )
        x = call(x, x, p, w_head, w_main, norm_pre[:, None, :], ssd_conv_w,
                 ssd_conv_b[:, None, :], dtb, a_row, dskip_row, ssd_norm[:, None, :], sc_conv_w,
                 e2, w_out_b, norm_post[:, None, :], w_gate_b, w_proj_b)
    return x
```

```python
import functools

import numpy as np
import jax
import jax.numpy as jnp
from jax import lax
from jax.experimental import pallas as pl
from jax.experimental.pallas import tpu as pltpu

F32 = jnp.float32
BF16 = jnp.bfloat16

EPS = 1e-6
CHUNK = 128
HEAD_DIM = 64
N_HEADS = 16
N_GROUPS = 2
HEADS_PER_GROUP = N_HEADS // N_GROUPS
D_STATE = 64
D_SSD = N_HEADS * HEAD_DIM
GN = N_GROUPS * D_STATE
XBC = D_SSD + 2 * GN
SSD_CONV_W = 4
SC_CONV_W = 3
LANES = 128
SUBLANES = 8
DT_PAD = LANES
QUAD = 4
SC_COL_BLOCK = 256

OFF_XBC = 0
OFF_DT = OFF_XBC + XBC
HEAD_COLS = OFF_DT + DT_PAD


def _sigmoid(v):
    return 0.5 + 0.5 * jnp.tanh(0.5 * v)


def _silu(v):
    h = 0.5 * v
    return h + h * jnp.tanh(h)


def _causal_taps(hist_and_body, taps):
    n_taps = taps.shape[0]
    acc = taps[n_taps - 1:n_taps, :] * hist_and_body[SUBLANES:, :]
    for k in range(n_taps - 1):
        shifted = pltpu.roll(hist_and_body, n_taps - 1 - k, axis=0)
        acc = acc + taps[k:k + 1, :] * shifted[SUBLANES:, :]
    return acc


def _softplus(v):
    return jnp.maximum(v, 0.0) + jnp.log1p(jnp.exp(-jnp.abs(v)))


def _layer_kernel(x_ref, xp_ref, p_ref, wh_ref, w_ref, npre_ref, cw_ref, cb_ref, dtb_ref, a_ref,
                  dskip_ref, snorm_ref, scw_ref, e2_ref, wout_ref, npost_ref, wg_ref, wp_ref,
                  o_ref,
                  hb_scr, u_scr, xbc_scr, dt_scr, z_scr, q_scr, s_scr, y_scr, x1b_scr,
                  acs_scr, acst_scr, dtt_scr, exp_scr,
                  *, tl, tiles_per_seq, d_mix_conv):
    n_chunks = tl // CHUNK
    off_sch = D_SSD
    off_scb = off_sch + d_mix_conv
    off_scc = off_scb + d_mix_conv
    off_zsc = off_scc + d_mix_conv
    step = pl.program_id(0)

    col_blocks = d_mix_conv // SC_COL_BLOCK
    cb_w = SC_COL_BLOCK
    n_out_blocks = o_ref.shape[-1] // cb_w

    def merge_out_proj(j, r):
        rows = slice(r * (tl // 2), (r + 1) * (tl // 2))
        cs = slice(j * cb_w, (j + 1) * cb_w)
        o_ref[rows, cs] = jnp.dot(y_scr[rows, :], wout_ref[:, cs], preferred_element_type=F32)

    def merge_norm():
        mix = o_ref[...]
        msm = jnp.mean(mix * mix, axis=-1, keepdims=True)
        x1 = xp_ref[...] + mix * lax.rsqrt(msm + EPS) * npost_ref[...]
        o_ref[...] = x1
        x1b_scr[...] = x1.astype(BF16)

    def merge_gate(j):
        cs = slice(j * cb_w, (j + 1) * cb_w)
        gate = _sigmoid(jnp.dot(x1b_scr[...], wg_ref[:, cs], preferred_element_type=F32))
        pp = jnp.dot(p_ref[...].astype(BF16), wp_ref[:, cs], preferred_element_type=F32)
        o_ref[:, cs] = o_ref[:, cs] + gate * pp

    def pre_norm():
        x = x_ref[...]
        ms = jnp.mean(x * x, axis=-1, keepdims=True)
        hb_scr[...] = (x * lax.rsqrt(ms + EPS) * npre_ref[...]).astype(BF16)

    def xbc_conv(j):
        cs = slice(j * cb_w, (j + 1) * cb_w)
        u_scr[SUBLANES:SUBLANES + tl, cs] = jnp.dot(hb_scr[...], wh_ref[:, cs],
                                                    preferred_element_type=F32)
        xbc_scr[:, cs] = _silu(cb_ref[:, cs] + _causal_taps(u_scr[:, cs], cw_ref[:, cs]))
        u_scr[0:SUBLANES, cs] = u_scr[tl:tl + SUBLANES, cs]

    def dt_proj():
        dt_raw = jnp.dot(hb_scr[...], wh_ref[:, OFF_DT:OFF_DT + DT_PAD],
                         preferred_element_type=F32)
        dt_scr[...] = _softplus(dt_raw + dtb_ref[...])

    def z_proj(j):
        cs = slice(j * cb_w, (j + 1) * cb_w)
        z_scr[:, cs] = jnp.dot(hb_scr[...], w_ref[:, cs], preferred_element_type=F32)

    def sc_piece(cblk, i):
        c0 = cblk * cb_w
        cs = slice(c0, c0 + cb_w)
        body = slice(SUBLANES, SUBLANES + tl)

        def proj(off):
            return jnp.dot(hb_scr[...], w_ref[:, off + c0:off + c0 + cb_w],
                           preferred_element_type=F32)

        if i == 0:
            q_scr[body, cs] = proj(off_scc)
        elif i == 1:
            q_scr[body, cs] = q_scr[body, cs] * proj(off_sch)
        elif i == 2:
            v = _causal_taps(q_scr[:, cs], scw_ref[:, cs])
            history = q_scr[tl:tl + SUBLANES, cs]
            q_scr[body, cs] = proj(off_scb) * v
            q_scr[0:SUBLANES, cs] = history
        else:
            y_sc = q_scr[body, cs] * _silu(proj(off_zsc))
            y_scr[:, D_SSD + c0:D_SSD + c0 + cb_w] = y_sc.astype(BF16)

    row_i = lax.broadcasted_iota(jnp.int32, (CHUNK, CHUNK), 0)
    col_i = lax.broadcasted_iota(jnp.int32, (CHUNK, CHUNK), 1)
    tril = row_i >= col_i
    tri_b = jnp.where(tril, 1.0, 0.0).astype(BF16)
    tri3 = jnp.concatenate([tri_b, tri_b, tri_b], axis=1)
    lane_q = lax.broadcasted_iota(jnp.int32, (1, QUAD * HEAD_DIM), 1) // HEAD_DIM
    quad_masks = [(lane_q == j).astype(BF16) for j in range(QUAD)]
    s_row = lax.broadcasted_iota(jnp.int32, (GN, D_SSD), 0) // D_STATE
    s_col = lax.broadcasted_iota(jnp.int32, (GN, D_SSD), 1) // (HEADS_PER_GROUP * HEAD_DIM)
    state_mask = s_row == s_col
    ck = {}

    def tile_prep():
        dt = dt_scr[...]
        adt = dt * a_ref[...]
        adt_l = jnp.concatenate([adt[c * CHUNK:(c + 1) * CHUNK] for c in range(n_chunks)], axis=1)
        a1 = adt_l.astype(BF16)
        r1 = adt_l - a1.astype(F32)
        a2 = r1.astype(BF16)
        a3 = (r1 - a2.astype(F32)).astype(BF16)
        acs_l = jnp.dot(tri3, jnp.concatenate([a1, a2, a3], axis=0), preferred_element_type=F32)
        factors = []
        for c in range(n_chunks):
            rs = slice(c * CHUNK, (c + 1) * CHUNK)
            acs = acs_l[:, rs]
            total = acs[CHUNK - 1:CHUNK, :]
            factors += [jnp.exp(total - acs) * dt[rs], jnp.exp(acs)]
            acs_scr[rs, :] = acs
            acst_scr[rs, :] = acs.T
            dtt_scr[rs, :] = dt[rs].T
        both = jnp.concatenate(factors, axis=0)
        hi = both.astype(BF16)
        lo = (both - hi.astype(F32)).astype(BF16)
        exp_scr[...] = jnp.dot(jnp.concatenate([hi, lo], axis=1), e2_ref[...],
                               preferred_element_type=F32)

    def chunk_prep(c):
        rs = slice(c * CHUNK, (c + 1) * CHUNK)
        xs = xbc_scr[rs, 0:D_SSD]
        bm = xbc_scr[rs, D_SSD:D_SSD + GN]
        cm = xbc_scr[rs, D_SSD + GN:XBC]
        bm_b = bm.astype(BF16)
        cbs = []
        for g in range(N_GROUPS):
            cm_g = jnp.where(col_i // D_STATE == g, cm, 0.0).astype(BF16)
            cbs.append(lax.dot_general(cm_g, bm_b, (((1,), (1,)), ((), ())),
                                       preferred_element_type=F32))
        ck.clear()
        ck.update(c=c, rs=rs, xs=xs, bm=bm, cm_b=cm.astype(BF16), xs_b=xs.astype(BF16), cbs=cbs,
                  y_parts=[])

    def chunk_quad(qd):
        c, rs = ck["c"], ck["rs"]
        ms_q = []
        for j in range(QUAD):
            h = qd * QUAD + j
            row = slice(c * CHUNK + h, c * CHUNK + h + 1)
            seg = acs_scr[rs, h:h + 1] - acst_scr[row, :]
            m = jnp.exp(jnp.where(tril, seg, -jnp.inf)) * ck["cbs"][h // HEADS_PER_GROUP]
            ms_q.append((m * dtt_scr[row, :]).astype(BF16))
        lhs = jnp.concatenate(ms_q, axis=1)
        xq = ck["xs_b"][:, qd * QUAD * HEAD_DIM:(qd + 1) * QUAD * HEAD_DIM]
        rhs = jnp.concatenate([xq * quad_masks[j] for j in range(QUAD)], axis=0)
        ck["y_parts"].append(jnp.dot(lhs, rhs, preferred_element_type=F32))

    def chunk_finish():
        c, rs, xs = ck["c"], ck["rs"], ck["xs"]
        wgt_e = exp_scr[2 * c * CHUNK:(2 * c + 1) * CHUNK, :]
        dout_e = exp_scr[(2 * c + 1) * CHUNK:(2 * c + 2) * CHUNK, :]
        y_diag = jnp.concatenate(ck["y_parts"], axis=1)
        s_old = s_scr[...]
        y_off = jnp.dot(ck["cm_b"], s_old.astype(BF16), preferred_element_type=F32)
        y = y_diag + dout_e * y_off + dskip_ref[...] * xs

        xw = (xs * wgt_e).astype(BF16)
        s_chunk = jnp.dot(ck["bm"].T.astype(BF16), xw, preferred_element_type=F32)
        s_scr[...] = s_old * dout_e[CHUNK - 1:CHUNK, :] + jnp.where(state_mask, s_chunk, 0.0)

        yz = y * _silu(z_scr[rs, :])
        gw = D_SSD // N_GROUPS
        for g in range(N_GROUPS):
            seg = yz[:, g * gw:(g + 1) * gw]
            msq = jnp.mean(seg * seg, axis=-1, keepdims=True)
            y_scr[rs, g * gw:(g + 1) * gw] = (
                seg * lax.rsqrt(msq + EPS) * snorm_ref[:, g * gw:(g + 1) * gw]).astype(BF16)

    @pl.when(step == 0)
    def _():
        y_scr[...] = jnp.zeros_like(y_scr)

    @pl.when(step % tiles_per_seq == 0)
    def _():
        u_scr[0:SUBLANES, :] = jnp.zeros((SUBLANES, XBC), F32)
        q_scr[0:SUBLANES, :] = jnp.zeros((SUBLANES, d_mix_conv), F32)
        s_scr[...] = jnp.zeros_like(s_scr)

    fillers = [functools.partial(merge_out_proj, j, r)
               for j in range(n_out_blocks) for r in range(2)]
    fillers += [functools.partial(z_proj, j) for j in range(D_SSD // cb_w)]
    n_before_finish = len(fillers)
    fillers += [merge_norm]
    fillers += [functools.partial(merge_gate, j) for j in range(n_out_blocks)]
    fillers += [functools.partial(sc_piece, cblk, i) for cblk in range(col_blocks) for i in range(4)]
    chain = [pre_norm] + [functools.partial(xbc_conv, j) for j in range(XBC // cb_w)]
    chain += [dt_proj, tile_prep]
    for c in range(n_chunks):
        chain += [functools.partial(chunk_prep, c)]
        chain += [functools.partial(chunk_quad, qd) for qd in range(N_HEADS // QUAD)]
        chain += [chunk_finish]

    fillers.reverse()
    for _ in range(max(n_before_finish - chain.index(chunk_finish), 1)):
        fillers.pop()()
    for piece in chain:
        piece()
        if fillers:
            fillers.pop()()
    while fillers:
        fillers.pop()()


def _layer_block(shape, layer):
    return pl.BlockSpec((None,) + tuple(shape), lambda s: (layer,) + (0,) * len(shape),
                        pipeline_mode=pl.Buffered(1))


def _expand_matrix():
    e = np.zeros((DT_PAD, D_SSD), np.float32)
    for h in range(N_HEADS):
        e[h, h * HEAD_DIM:(h + 1) * HEAD_DIM] = 1.0
    return jnp.asarray(np.concatenate([e, e], axis=0), BF16)


def _weight_split_kernel(w_ref, head_ref, main_ref):
    w = w_ref[...]
    n_dt = OFF_DT + N_HEADS
    lane = lax.broadcasted_iota(jnp.int32, (w.shape[0], HEAD_COLS), 1)
    head_ref[...] = jnp.where(lane < n_dt, w[:, 0:HEAD_COLS], 0.0).astype(BF16)
    main_ref[...] = w[:, n_dt:].astype(BF16)


def _split_input_weight(w_in):
    depth, d_model, cols = w_in.shape
    main_cols = cols - (OFF_DT + N_HEADS)
    rows = min(WEIGHT_PREP_ROWS, d_model)
    assert d_model % rows == 0
    return pl.pallas_call(
        _weight_split_kernel,
        grid=(depth, d_model // rows),
        in_specs=[pl.BlockSpec((None, rows, cols), lambda l, r: (l, r, 0))],
        out_specs=[pl.BlockSpec((None, rows, HEAD_COLS), lambda l, r: (l, r, 0)),
                   pl.BlockSpec((None, rows, main_cols), lambda l, r: (l, r, 0))],
        out_shape=[jax.ShapeDtypeStruct((depth, d_model, HEAD_COLS), BF16),
                   jax.ShapeDtypeStruct((depth, d_model, main_cols), BF16)],
        name="weight_split",
    )(w_in)


SEQ_TILE = 512
WEIGHT_PREP_ROWS = 256
VMEM_LIMIT = 58 * 1024 * 1024


def kernel(x, p, norm_pre, norm_post, w_in, ssd_conv_w, ssd_conv_b, dt_bias, a_log, d_skip,
           ssd_norm, sc_conv_w, w_out, w_ple_gate, w_ple_proj):
    depth, bsz, length, ple_dim = p.shape
    d_model = x.shape[-1]
    d_mix_conv = sc_conv_w.shape[-1]
    d_mix = D_SSD + d_mix_conv
    tl = min(SEQ_TILE, length)
    assert length % tl == 0 and tl % CHUNK == 0 and d_mix_conv % SC_COL_BLOCK == 0
    tiles_per_seq = length // tl
    n_tiles = bsz * tiles_per_seq

    w_head, w_main = _split_input_weight(w_in)
    main_cols = w_main.shape[-1]
    pad = ((0, 0), (0, DT_PAD - N_HEADS))
    dtb = jnp.pad(dt_bias, pad)[:, None, :]
    a_row = jnp.pad(-jnp.exp(a_log), pad)[:, None, :]
    dskip_row = jnp.repeat(d_skip, HEAD_DIM, axis=1)[:, None, :]
    w_out_b = w_out.astype(BF16)
    w_gate_b = w_ple_gate.astype(BF16)
    w_proj_b = w_ple_proj.astype(BF16)
    e2 = _expand_matrix()

    def cur_tile(s):
        n = jnp.minimum(s, n_tiles - 1)
        return n // tiles_per_seq, n % tiles_per_seq

    def prev_tile(s):
        n = jnp.maximum(s - 1, 0)
        return n // tiles_per_seq, n % tiles_per_seq

    def ple_tile(layer, s):
        return (layer,) + prev_tile(s) + (0,)

    for layer in range(depth):
        call = pl.pallas_call(
            functools.partial(_layer_kernel, tl=tl, tiles_per_seq=tiles_per_seq,
                              d_mix_conv=d_mix_conv),
            grid=(n_tiles + 1,),
            in_specs=[
                pl.BlockSpec((None, tl, d_model), lambda s: cur_tile(s) + (0,)),
                pl.BlockSpec((None, tl, d_model), lambda s: prev_tile(s) + (0,)),
                pl.BlockSpec((None, None, tl, ple_dim), functools.partial(ple_tile, layer)),
                _layer_block((d_model, HEAD_COLS), layer),
                _layer_block((d_model, main_cols), layer),
                _layer_block((1, d_model), layer),
                _layer_block((SSD_CONV_W, XBC), layer),
                _layer_block((1, XBC), layer),
                _layer_block((1, DT_PAD), layer),
                _layer_block((1, DT_PAD), layer),
                _layer_block((1, D_SSD), layer),
                _layer_block((1, D_SSD), layer),
                _layer_block((SC_CONV_W, d_mix_conv), layer),
                pl.BlockSpec((2 * DT_PAD, D_SSD), lambda s: (0, 0), pipeline_mode=pl.Buffered(1)),
                _layer_block((d_mix, d_model), layer),
                _layer_block((1, d_model), layer),
                _layer_block((d_model, d_model), layer),
                _layer_block((ple_dim, d_model), layer),
            ],
            out_specs=pl.BlockSpec((None, tl, d_model), lambda s: prev_tile(s) + (0,)),
            out_shape=jax.ShapeDtypeStruct((bsz, length, d_model), F32),
            scratch_shapes=[
                pltpu.VMEM((tl, d_model), BF16),
                pltpu.VMEM((tl + SUBLANES, XBC), F32),
                pltpu.VMEM((tl, XBC), F32),
                pltpu.VMEM((tl, DT_PAD), F32),
                pltpu.VMEM((tl, D_SSD), F32),
                pltpu.VMEM((tl + SUBLANES, d_mix_conv), F32),
                pltpu.VMEM((GN, D_SSD), F32),
                pltpu.VMEM((tl, d_mix), BF16),
                pltpu.VMEM((tl, d_model), BF16),
                pltpu.VMEM((tl, DT_PAD), F32),
                pltpu.VMEM((tl, CHUNK), F32),
                pltpu.VMEM((tl, CHUNK), F32),
                pltpu.VMEM((2 * tl, D_SSD), F32),
            ],
            compiler_params=pltpu.CompilerParams(
                dimension_semantics=("arbitrary",), vmem_limit_bytes=VMEM_LIMIT),
            name="layer",
        )
        x = call(x, x, p, w_head, w_main, norm_pre[:, None, :], ssd_conv_w,
                 ssd_conv_b[:, None, :], dtb, a_row, dskip_row, ssd_norm[:, None, :], sc_conv_w,
                 e2, w_out_b, norm_post[:, None, :], w_gate_b, w_proj_b)
    return x
```

```python
import functools

import numpy as np
import jax
import jax.numpy as jnp
from jax import lax
from jax.experimental import pallas as pl
from jax.experimental.pallas import tpu as pltpu

F32 = jnp.float32
BF16 = jnp.bfloat16

EPS = 1e-6
CHUNK = 128
HEAD_DIM = 64
N_HEADS = 16
N_GROUPS = 2
HEADS_PER_GROUP = N_HEADS // N_GROUPS
D_STATE = 64
D_SSD = N_HEADS * HEAD_DIM
GN = N_GROUPS * D_STATE
XBC = D_SSD + 2 * GN
SSD_CONV_W = 4
SC_CONV_W = 3
LANES = 128
SUBLANES = 8
DT_PAD = LANES
QUAD = 4
SC_COL_BLOCK = 256

OFF_XBC = 0
OFF_DT = OFF_XBC + XBC
HEAD_COLS = OFF_DT + DT_PAD


def _sigmoid(v):
    return 0.5 + 0.5 * jnp.tanh(0.5 * v)


def _silu(v):
    h = 0.5 * v
    return h + h * jnp.tanh(h)


def _causal_taps(hist_and_body, taps):
    n_taps = taps.shape[0]
    acc = taps[n_taps - 1:n_taps, :] * hist_and_body[SUBLANES:, :]
    for k in range(n_taps - 1):
        shifted = pltpu.roll(hist_and_body, n_taps - 1 - k, axis=0)
        acc = acc + taps[k:k + 1, :] * shifted[SUBLANES:, :]
    return acc


def _softplus(v):
    return jnp.maximum(v, 0.0) + jnp.log1p(jnp.exp(-jnp.abs(v)))


def _layer_kernel(x_ref, xp_ref, p_ref, wh_ref, w_ref, npre_ref, cw_ref, cb_ref, dtb_ref, a_ref,
                  dskip_ref, snorm_ref, scw_ref, e2_ref, wout_ref, npost_ref, wg_ref, wp_ref,
                  o_ref,
                  hb_scr, u_scr, xbc_scr, dt_scr, z_scr, q_scr, s_scr, y_scr, x1b_scr,
                  acs_scr, acst_scr, dtt_scr, exp_scr,
                  *, tl, tiles_per_seq, d_mix_conv):
    n_chunks = tl // CHUNK
    off_sch = D_SSD
    off_scb = off_sch + d_mix_conv
    off_scc = off_scb + d_mix_conv
    off_zsc = off_scc + d_mix_conv
    step = pl.program_id(0)

    col_blocks = d_mix_conv // SC_COL_BLOCK
    cb_w = SC_COL_BLOCK
    n_out_blocks = o_ref.shape[-1] // cb_w

    def merge_out_proj(j, r):
        rows = slice(r * (tl // 2), (r + 1) * (tl // 2))
        cs = slice(j * cb_w, (j + 1) * cb_w)
        o_ref[rows, cs] = jnp.dot(y_scr[rows, :], wout_ref[:, cs], preferred_element_type=F32)

    def merge_norm():
        mix = o_ref[...]
        msm = jnp.mean(mix * mix, axis=-1, keepdims=True)
        x1 = xp_ref[...] + mix * lax.rsqrt(msm + EPS) * npost_ref[...]
        o_ref[...] = x1
        x1b_scr[...] = x1.astype(BF16)

    def merge_gate(j):
        cs = slice(j * cb_w, (j + 1) * cb_w)
        gate = _sigmoid(jnp.dot(x1b_scr[...], wg_ref[:, cs], preferred_element_type=F32))
        pp = jnp.dot(p_ref[...].astype(BF16), wp_ref[:, cs], preferred_element_type=F32)
        o_ref[:, cs] = o_ref[:, cs] + gate * pp

    def pre_norm():
        x = x_ref[...]
        ms = jnp.mean(x * x, axis=-1, keepdims=True)
        hb_scr[...] = (x * lax.rsqrt(ms + EPS) * npre_ref[...]).astype(BF16)

    def xbc_conv(j):
        cs = slice(j * cb_w, (j + 1) * cb_w)
        u = jnp.dot(hb_scr[...], wh_ref[:, cs], preferred_element_type=F32)
        hist_and_body = jnp.concatenate([u_scr[:, cs], u], axis=0)
        xbc_scr[:, cs] = _silu(cb_ref[:, cs] + _causal_taps(hist_and_body, cw_ref[:, cs]))
        u_scr[:, cs] = u[tl - SUBLANES:tl, :]

    def dt_proj():
        dt_raw = jnp.dot(hb_scr[...], wh_ref[:, OFF_DT:OFF_DT + DT_PAD],
                         preferred_element_type=F32)
        dt_scr[...] = _softplus(dt_raw + dtb_ref[...])

    def z_proj(j):
        cs = slice(j * cb_w, (j + 1) * cb_w)
        z_scr[:, cs] = _silu(jnp.dot(hb_scr[...], w_ref[:, cs], preferred_element_type=F32))

    def sc_piece(cblk, i):
        c0 = cblk * cb_w
        cs = slice(c0, c0 + cb_w)
        body = slice(SUBLANES, SUBLANES + tl)

        def proj(off):
            return jnp.dot(hb_scr[...], w_ref[:, off + c0:off + c0 + cb_w],
                           preferred_element_type=F32)

        if i == 0:
            q_scr[body, cs] = proj(off_scc)
        elif i == 1:
            q_scr[body, cs] = q_scr[body, cs] * proj(off_sch)
        elif i == 2:
            v = _causal_taps(q_scr[:, cs], scw_ref[:, cs])
            history = q_scr[tl:tl + SUBLANES, cs]
            q_scr[body, cs] = proj(off_scb) * v
            q_scr[0:SUBLANES, cs] = history
        else:
            y_sc = q_scr[body, cs] * _silu(proj(off_zsc))
            y_scr[:, D_SSD + c0:D_SSD + c0 + cb_w] = y_sc.astype(BF16)

    row_i = lax.broadcasted_iota(jnp.int32, (CHUNK, CHUNK), 0)
    col_i = lax.broadcasted_iota(jnp.int32, (CHUNK, CHUNK), 1)
    tril = row_i >= col_i
    tri_b = jnp.where(tril, 1.0, 0.0).astype(BF16)
    tri3 = jnp.concatenate([tri_b, tri_b, tri_b], axis=1)
    lane_q = lax.broadcasted_iota(jnp.int32, (1, QUAD * HEAD_DIM), 1) // HEAD_DIM
    quad_masks = [(lane_q == j).astype(BF16) for j in range(QUAD)]
    s_row = lax.broadcasted_iota(jnp.int32, (GN, D_SSD), 0) // D_STATE
    s_col = lax.broadcasted_iota(jnp.int32, (GN, D_SSD), 1) // (HEADS_PER_GROUP * HEAD_DIM)
    state_mask = s_row == s_col
    ck = {}

    def tile_prep():
        dt = dt_scr[...]
        adt = dt * a_ref[...]
        adt_l = jnp.concatenate([adt[c * CHUNK:(c + 1) * CHUNK] for c in range(n_chunks)], axis=1)
        a1 = adt_l.astype(BF16)
        r1 = adt_l - a1.astype(F32)
        a2 = r1.astype(BF16)
        a3 = (r1 - a2.astype(F32)).astype(BF16)
        acs_l = jnp.dot(tri3, jnp.concatenate([a1, a2, a3], axis=0), preferred_element_type=F32)
        factors = []
        for c in range(n_chunks):
            rs = slice(c * CHUNK, (c + 1) * CHUNK)
            acs = acs_l[:, rs]
            total = acs[CHUNK - 1:CHUNK, :]
            factors += [jnp.exp(total - acs) * dt[rs], jnp.exp(acs)]
            acs_scr[rs, :] = acs
            acst_scr[rs, :] = acs.T
            dtt_scr[rs, :] = dt[rs].T
        both = jnp.concatenate(factors, axis=0)
        hi = both.astype(BF16)
        lo = (both - hi.astype(F32)).astype(BF16)
        exp_scr[...] = jnp.dot(jnp.concatenate([hi, lo], axis=1), e2_ref[...],
                               preferred_element_type=F32)

    def chunk_prep(c):
        rs = slice(c * CHUNK, (c + 1) * CHUNK)
        xs = xbc_scr[rs, 0:D_SSD]
        bm = xbc_scr[rs, D_SSD:D_SSD + GN]
        cm = xbc_scr[rs, D_SSD + GN:XBC]
        bm_b = bm.astype(BF16)
        cbs = []
        for g in range(N_GROUPS):
            cm_g = jnp.where(col_i // D_STATE == g, cm, 0.0).astype(BF16)
            cbs.append(lax.dot_general(cm_g, bm_b, (((1,), (1,)), ((), ())),
                                       preferred_element_type=F32))
        ck.clear()
        ck.update(c=c, rs=rs, xs=xs, bm=bm, cm_b=cm.astype(BF16), xs_b=xs.astype(BF16), cbs=cbs,
                  y_parts=[])

    def chunk_quad(qd):
        c, rs = ck["c"], ck["rs"]
        ms_q = []
        for j in range(QUAD):
            h = qd * QUAD + j
            row = slice(c * CHUNK + h, c * CHUNK + h + 1)
            seg = acs_scr[rs, h:h + 1] - acst_scr[row, :]
            m = jnp.exp(jnp.where(tril, seg, -jnp.inf)) * ck["cbs"][h // HEADS_PER_GROUP]
            ms_q.append((m * dtt_scr[row, :]).astype(BF16))
        lhs = jnp.concatenate(ms_q, axis=1)
        xq = ck["xs_b"][:, qd * QUAD * HEAD_DIM:(qd + 1) * QUAD * HEAD_DIM]
        rhs = jnp.concatenate([xq * quad_masks[j] for j in range(QUAD)], axis=0)
        ck["y_parts"].append(jnp.dot(lhs, rhs, preferred_element_type=F32))

    def chunk_finish():
        c, rs, xs = ck["c"], ck["rs"], ck["xs"]
        wgt_e = exp_scr[2 * c * CHUNK:(2 * c + 1) * CHUNK, :]
        dout_e = exp_scr[(2 * c + 1) * CHUNK:(2 * c + 2) * CHUNK, :]
        y_diag = jnp.concatenate(ck["y_parts"], axis=1)
        s_old = s_scr[...]
        y_off = jnp.dot(ck["cm_b"], s_old.astype(BF16), preferred_element_type=F32)
        y = y_diag + dout_e * y_off + dskip_ref[...] * xs

        xw = (xs * wgt_e).astype(BF16)
        s_chunk = jnp.dot(ck["bm"].T.astype(BF16), xw, preferred_element_type=F32)
        s_scr[...] = s_old * dout_e[CHUNK - 1:CHUNK, :] + jnp.where(state_mask, s_chunk, 0.0)

        yz = y * z_scr[rs, :]
        gw = D_SSD // N_GROUPS
        for g in range(N_GROUPS):
            seg = yz[:, g * gw:(g + 1) * gw]
            msq = jnp.mean(seg * seg, axis=-1, keepdims=True)
            y_scr[rs, g * gw:(g + 1) * gw] = (
                seg * lax.rsqrt(msq + EPS) * snorm_ref[:, g * gw:(g + 1) * gw]).astype(BF16)

    @pl.when(step == 0)
    def _():
        y_scr[...] = jnp.zeros_like(y_scr)

    @pl.when(step % tiles_per_seq == 0)
    def _():
        u_scr[...] = jnp.zeros_like(u_scr)
        q_scr[0:SUBLANES, :] = jnp.zeros((SUBLANES, d_mix_conv), F32)
        s_scr[...] = jnp.zeros_like(s_scr)

    fillers = [functools.partial(merge_out_proj, j, r)
               for j in range(n_out_blocks) for r in range(2)]
    fillers += [functools.partial(z_proj, j) for j in range(D_SSD // cb_w)]
    n_before_finish = len(fillers)
    fillers += [merge_norm]
    fillers += [functools.partial(merge_gate, j) for j in range(n_out_blocks)]
    fillers += [functools.partial(sc_piece, cblk, i) for cblk in range(col_blocks) for i in range(4)]
    chain = [pre_norm] + [functools.partial(xbc_conv, j) for j in range(XBC // cb_w)]
    chain += [dt_proj, tile_prep]
    for c in range(n_chunks):
        chain += [functools.partial(chunk_prep, c)]
        chain += [functools.partial(chunk_quad, qd) for qd in range(N_HEADS // QUAD)]
        chain += [chunk_finish]

    fillers.reverse()
    for _ in range(max(n_before_finish - chain.index(chunk_finish), 1)):
        fillers.pop()()
    for piece in chain:
        piece()
        if fillers:
            fillers.pop()()
    while fillers:
        fillers.pop()()


def _layer_block(shape, layer):
    return pl.BlockSpec((None,) + tuple(shape), lambda s: (layer,) + (0,) * len(shape),
                        pipeline_mode=pl.Buffered(1))


def _expand_matrix():
    e = np.zeros((DT_PAD, D_SSD), np.float32)
    for h in range(N_HEADS):
        e[h, h * HEAD_DIM:(h + 1) * HEAD_DIM] = 1.0
    return jnp.asarray(np.concatenate([e, e], axis=0), BF16)


def _weight_split_kernel(w_ref, head_ref, main_ref):
    w = w_ref[...]
    n_dt = OFF_DT + N_HEADS
    lane = lax.broadcasted_iota(jnp.int32, (w.shape[0], HEAD_COLS), 1)
    head_ref[...] = jnp.where(lane < n_dt, w[:, 0:HEAD_COLS], 0.0).astype(BF16)
    main_ref[...] = w[:, n_dt:].astype(BF16)


def _split_input_weight(w_in):
    depth, d_model, cols = w_in.shape
    main_cols = cols - (OFF_DT + N_HEADS)
    rows = min(WEIGHT_PREP_ROWS, d_model)
    assert d_model % rows == 0
    return pl.pallas_call(
        _weight_split_kernel,
        grid=(depth, d_model // rows),
        in_specs=[pl.BlockSpec((None, rows, cols), lambda l, r: (l, r, 0))],
        out_specs=[pl.BlockSpec((None, rows, HEAD_COLS), lambda l, r: (l, r, 0)),
                   pl.BlockSpec((None, rows, main_cols), lambda l, r: (l, r, 0))],
        out_shape=[jax.ShapeDtypeStruct((depth, d_model, HEAD_COLS), BF16),
                   jax.ShapeDtypeStruct((depth, d_model, main_cols), BF16)],
        name="weight_split",
    )(w_in)


SEQ_TILE = 512
WEIGHT_PREP_ROWS = 256
VMEM_LIMIT = 58 * 1024 * 1024


def kernel(x, p, norm_pre, norm_post, w_in, ssd_conv_w, ssd_conv_b, dt_bias, a_log, d_skip,
           ssd_norm, sc_conv_w, w_out, w_ple_gate, w_ple_proj):
    depth, bsz, length, ple_dim = p.shape
    d_model = x.shape[-1]
    d_mix_conv = sc_conv_w.shape[-1]
    d_mix = D_SSD + d_mix_conv
    tl = min(SEQ_TILE, length)
    assert length % tl == 0 and tl % CHUNK == 0 and d_mix_conv % SC_COL_BLOCK == 0
    tiles_per_seq = length // tl
    n_tiles = bsz * tiles_per_seq

    w_head, w_main = _split_input_weight(w_in)
    main_cols = w_main.shape[-1]
    pad = ((0, 0), (0, DT_PAD - N_HEADS))
    dtb = jnp.pad(dt_bias, pad)[:, None, :]
    a_row = jnp.pad(-jnp.exp(a_log), pad)[:, None, :]
    dskip_row = jnp.repeat(d_skip, HEAD_DIM, axis=1)[:, None, :]
    w_out_b = w_out.astype(BF16)
    w_gate_b = w_ple_gate.astype(BF16)
    w_proj_b = w_ple_proj.astype(BF16)
    e2 = _expand_matrix()

    def cur_tile(s):
        n = jnp.minimum(s, n_tiles - 1)
        return n // tiles_per_seq, n % tiles_per_seq

    def prev_tile(s):
        n = jnp.maximum(s - 1, 0)
        return n // tiles_per_seq, n % tiles_per_seq

    def ple_tile(layer, s):
        return (layer,) + prev_tile(s) + (0,)

    for layer in range(depth):
        call = pl.pallas_call(
            functools.partial(_layer_kernel, tl=tl, tiles_per_seq=tiles_per_seq,
                              d_mix_conv=d_mix_conv),
            grid=(n_tiles + 1,),
            in_specs=[
                pl.BlockSpec((None, tl, d_model), lambda s: cur_tile(s) + (0,)),
                pl.BlockSpec((None, tl, d_model), lambda s: prev_tile(s) + (0,)),
                pl.BlockSpec((None, None, tl, ple_dim), functools.partial(ple_tile, layer)),
                _layer_block((d_model, HEAD_COLS), layer),
                _layer_block((d_model, main_cols), layer),
                _layer_block((1, d_model), layer),
                _layer_block((SSD_CONV_W, XBC), layer),
                _layer_block((1, XBC), layer),
                _layer_block((1, DT_PAD), layer),
                _layer_block((1, DT_PAD), layer),
                _layer_block((1, D_SSD), layer),
                _layer_block((1, D_SSD), layer),
                _layer_block((SC_CONV_W, d_mix_conv), layer),
                pl.BlockSpec((2 * DT_PAD, D_SSD), lambda s: (0, 0), pipeline_mode=pl.Buffered(1)),
                _layer_block((d_mix, d_model), layer),
                _layer_block((1, d_model), layer),
                _layer_block((d_model, d_model), layer),
                _layer_block((ple_dim, d_model), layer),
            ],
            out_specs=pl.BlockSpec((None, tl, d_model), lambda s: prev_tile(s) + (0,)),
            out_shape=jax.ShapeDtypeStruct((bsz, length, d_model), F32),
            scratch_shapes=[
                pltpu.VMEM((tl, d_model), BF16),
                pltpu.VMEM((SUBLANES, XBC), F32),
                pltpu.VMEM((tl, XBC), F32),
                pltpu.VMEM((tl, DT_PAD), F32),
                pltpu.VMEM((tl, D_SSD), F32),
                pltpu.VMEM((tl + SUBLANES, d_mix_conv), F32),
                pltpu.VMEM((GN, D_SSD), F32),
                pltpu.VMEM((tl, d_mix), BF16),
                pltpu.VMEM((tl, d_model), BF16),
                pltpu.VMEM((tl, DT_PAD), F32),
                pltpu.VMEM((tl, CHUNK), F32),
                pltpu.VMEM((tl, CHUNK), F32),
                pltpu.VMEM((2 * tl, D_SSD), F32),
            ],
            compiler_params=pltpu.CompilerParams(
                dimension_semantics=("arbitrary",), vmem_limit_bytes=VMEM_LIMIT),
            name="layer",
        )
        x = call(x, x, p, w_head, w_main, norm_pre[:, None, :], ssd_conv_w,
                 ssd_conv_b[:, None, :], dtb, a_row, dskip_row, ssd_norm[:, None, :], sc_conv_w,
                 e2, w_out_b, norm_post[:, None, :], w_gate_b, w_proj_b)
    return x
```

```python
import functools

import numpy as np
import jax
import jax.numpy as jnp
from jax import lax
from jax.experimental import pallas as pl
from jax.experimental.pallas import tpu as pltpu

F32 = jnp.float32
BF16 = jnp.bfloat16

EPS = 1e-6
CHUNK = 128
HEAD_DIM = 64
N_HEADS = 16
N_GROUPS = 2
HEADS_PER_GROUP = N_HEADS // N_GROUPS
D_STATE = 64
D_SSD = N_HEADS * HEAD_DIM
GN = N_GROUPS * D_STATE
XBC = D_SSD + 2 * GN
SSD_CONV_W = 4
SC_CONV_W = 3
LANES = 128
SUBLANES = 8
DT_PAD = LANES
QUAD = 4
SC_COL_BLOCK = 256

OFF_XBC = 0
OFF_DT = OFF_XBC + XBC
HEAD_COLS = OFF_DT + DT_PAD


def _sigmoid(v):
    return 0.5 + 0.5 * jnp.tanh(0.5 * v)


def _silu(v):
    h = 0.5 * v
    return h + h * jnp.tanh(h)


def _causal_taps(hist_and_body, taps):
    n_taps = taps.shape[0]
    acc = taps[n_taps - 1:n_taps, :] * hist_and_body[SUBLANES:, :]
    for k in range(n_taps - 1):
        shifted = pltpu.roll(hist_and_body, n_taps - 1 - k, axis=0)
        acc = acc + taps[k:k + 1, :] * shifted[SUBLANES:, :]
    return acc


def _softplus(v):
    return jnp.maximum(v, 0.0) + jnp.log1p(jnp.exp(-jnp.abs(v)))


def _layer_kernel(x_ref, xp_ref, p_ref, wh_ref, w_ref, npre_ref, cw_ref, cb_ref, dtb_ref, a_ref,
                  dskip_ref, snorm_ref, scw_ref, e2_ref, wout_ref, npost_ref, wg_ref, wp_ref,
                  o_ref,
                  hb_scr, u_scr, xbc_scr, dt_scr, z_scr, q_scr, s_scr, y_scr, x1b_scr,
                  acs_scr, acst_scr, dtt_scr, exp_scr,
                  *, tl, tiles_per_seq, d_mix_conv):
    n_chunks = tl // CHUNK
    off_sch = D_SSD
    off_scb = off_sch + d_mix_conv
    off_scc = off_scb + d_mix_conv
    off_zsc = off_scc + d_mix_conv
    step = pl.program_id(0)

    col_blocks = d_mix_conv // SC_COL_BLOCK
    cb_w = SC_COL_BLOCK
    n_out_blocks = o_ref.shape[-1] // cb_w

    def merge_out_proj(j, r):
        rows = slice(r * (tl // 2), (r + 1) * (tl // 2))
        cs = slice(j * cb_w, (j + 1) * cb_w)
        o_ref[rows, cs] = jnp.dot(y_scr[rows, :], wout_ref[:, cs], preferred_element_type=F32)

    def merge_norm():
        mix = o_ref[...]
        msm = jnp.mean(mix * mix, axis=-1, keepdims=True)
        x1 = xp_ref[...] + mix * lax.rsqrt(msm + EPS) * npost_ref[...]
        o_ref[...] = x1
        x1b_scr[...] = x1.astype(BF16)

    def merge_gate(j):
        cs = slice(j * cb_w, (j + 1) * cb_w)
        gate = _sigmoid(jnp.dot(x1b_scr[...], wg_ref[:, cs], preferred_element_type=F32))
        pp = jnp.dot(p_ref[...].astype(BF16), wp_ref[:, cs], preferred_element_type=F32)
        o_ref[:, cs] = o_ref[:, cs] + gate * pp

    def pre_norm():
        x = x_ref[...]
        ms = jnp.mean(x * x, axis=-1, keepdims=True)
        hb_scr[...] = (x * lax.rsqrt(ms + EPS) * npre_ref[...]).astype(BF16)

    def xbc_conv(j):
        cs = slice(j * cb_w, (j + 1) * cb_w)
        u_scr[SUBLANES:SUBLANES + tl, cs] = jnp.dot(hb_scr[...], wh_ref[:, cs],
                                                    preferred_element_type=F32)
        xbc_scr[:, cs] = _silu(cb_ref[:, cs] + _causal_taps(u_scr[:, cs], cw_ref[:, cs]))
        u_scr[0:SUBLANES, cs] = u_scr[tl:tl + SUBLANES, cs]

    def dt_proj():
        dt_raw = jnp.dot(hb_scr[...], wh_ref[:, OFF_DT:OFF_DT + DT_PAD],
                         preferred_element_type=F32)
        dt_scr[...] = _softplus(dt_raw + dtb_ref[...])

    def z_proj(j):
        cs = slice(j * cb_w, (j + 1) * cb_w)
        z_scr[:, cs] = jnp.dot(hb_scr[...], w_ref[:, cs], preferred_element_type=F32)

    def sc_piece(cblk, i):
        c0 = cblk * cb_w
        cs = slice(c0, c0 + cb_w)
        body = slice(SUBLANES, SUBLANES + tl)

        def proj(off):
            return jnp.dot(hb_scr[...], w_ref[:, off + c0:off + c0 + cb_w],
                           preferred_element_type=F32)

        if i == 0:
            q_scr[body, cs] = proj(off_scc)
        elif i == 1:
            q_scr[body, cs] = q_scr[body, cs] * proj(off_sch)
        elif i == 2:
            v = _causal_taps(q_scr[:, cs], scw_ref[:, cs])
            history = q_scr[tl:tl + SUBLANES, cs]
            q_scr[body, cs] = proj(off_scb) * v
            q_scr[0:SUBLANES, cs] = history
        else:
            y_sc = q_scr[body, cs] * _silu(proj(off_zsc))
            y_scr[:, D_SSD + c0:D_SSD + c0 + cb_w] = y_sc.astype(BF16)

    row_i = lax.broadcasted_iota(jnp.int32, (CHUNK, CHUNK), 0)
    col_i = lax.broadcasted_iota(jnp.int32, (CHUNK, CHUNK), 1)
    tril = row_i >= col_i
    tri_b = jnp.where(tril, 1.0, 0.0).astype(BF16)
    tri3 = jnp.concatenate([tri_b, tri_b, tri_b], axis=1)
    lane_q = lax.broadcasted_iota(jnp.int32, (1, QUAD * HEAD_DIM), 1) // HEAD_DIM
    quad_masks = [(lane_q == j).astype(BF16) for j in range(QUAD)]
    s_row = lax.broadcasted_iota(jnp.int32, (GN, D_SSD), 0) // D_STATE
    s_col = lax.broadcasted_iota(jnp.int32, (GN, D_SSD), 1) // (HEADS_PER_GROUP * HEAD_DIM)
    state_mask = s_row == s_col
    ck = {}

    def tile_prep():
        dt = dt_scr[...]
        adt = dt * a_ref[...]
        adt_l = jnp.concatenate([adt[c * CHUNK:(c + 1) * CHUNK] for c in range(n_chunks)], axis=1)
        a1 = adt_l.astype(BF16)
        r1 = adt_l - a1.astype(F32)
        a2 = r1.astype(BF16)
        a3 = (r1 - a2.astype(F32)).astype(BF16)
        acs_l = jnp.dot(tri3, jnp.concatenate([a1, a2, a3], axis=0), preferred_element_type=F32)
        factors = []
        for c in range(n_chunks):
            rs = slice(c * CHUNK, (c + 1) * CHUNK)
            acs = acs_l[:, rs]
            total = acs[CHUNK - 1:CHUNK, :]
            factors += [jnp.exp(total - acs) * dt[rs], jnp.exp(acs)]
            acs_scr[rs, :] = acs
            acst_scr[rs, :] = acs.T
            dtt_scr[rs, :] = dt[rs].T
        both = jnp.concatenate(factors, axis=0)
        hi = both.astype(BF16)
        lo = (both - hi.astype(F32)).astype(BF16)
        exp_scr[...] = jnp.dot(jnp.concatenate([hi, lo], axis=1), e2_ref[...],
                               preferred_element_type=F32)

    def chunk_prep(c):
        rs = slice(c * CHUNK, (c + 1) * CHUNK)
        xs = xbc_scr[rs, 0:D_SSD]
        bm = xbc_scr[rs, D_SSD:D_SSD + GN]
        cm = xbc_scr[rs, D_SSD + GN:XBC]
        bm_b = bm.astype(BF16)
        cbs = []
        for g in range(N_GROUPS):
            cm_g = jnp.where(col_i // D_STATE == g, cm, 0.0).astype(BF16)
            cbs.append(lax.dot_general(cm_g, bm_b, (((1,), (1,)), ((), ())),
                                       preferred_element_type=F32))
        ck.clear()
        ck.update(c=c, rs=rs, xs=xs, bm=bm, cm_b=cm.astype(BF16), xs_b=xs.astype(BF16), cbs=cbs,
                  y_parts=[])

    def chunk_quad(qd):
        c, rs = ck["c"], ck["rs"]
        ms_q = []
        for j in range(QUAD):
            h = qd * QUAD + j
            row = slice(c * CHUNK + h, c * CHUNK + h + 1)
            seg = acs_scr[rs, h:h + 1] - acst_scr[row, :]
            m = jnp.exp(jnp.where(tril, seg, -jnp.inf)) * ck["cbs"][h // HEADS_PER_GROUP]
            ms_q.append((m * dtt_scr[row, :]).astype(BF16))
        lhs = jnp.concatenate(ms_q, axis=1)
        xq = ck["xs_b"][:, qd * QUAD * HEAD_DIM:(qd + 1) * QUAD * HEAD_DIM]
        rhs = jnp.concatenate([xq * quad_masks[j] for j in range(QUAD)], axis=0)
        ck["y_parts"].append(jnp.dot(lhs, rhs, preferred_element_type=F32))

    def chunk_finish():
        c, rs, xs = ck["c"], ck["rs"], ck["xs"]
        wgt_e = exp_scr[2 * c * CHUNK:(2 * c + 1) * CHUNK, :]
        dout_e = exp_scr[(2 * c + 1) * CHUNK:(2 * c + 2) * CHUNK, :]
        y_diag = jnp.concatenate(ck["y_parts"], axis=1)
        s_old = s_scr[...]
        y_off = jnp.dot(ck["cm_b"], s_old.astype(BF16), preferred_element_type=F32)
        y = y_diag + dout_e * y_off + dskip_ref[...] * xs

        xw = (xs * wgt_e).astype(BF16)
        s_chunk = jnp.dot(ck["bm"].T.astype(BF16), xw, preferred_element_type=F32)
        s_scr[...] = s_old * dout_e[CHUNK - 1:CHUNK, :] + jnp.where(state_mask, s_chunk, 0.0)

        yz = y * _silu(z_scr[rs, :])
        gw = D_SSD // N_GROUPS
        for g in range(N_GROUPS):
            seg = yz[:, g * gw:(g + 1) * gw]
            msq = jnp.mean(seg * seg, axis=-1, keepdims=True)
            y_scr[rs, g * gw:(g + 1) * gw] = (
                seg * lax.rsqrt(msq + EPS) * snorm_ref[:, g * gw:(g + 1) * gw]).astype(BF16)

    @pl.when(step == 0)
    def _():
        y_scr[...] = jnp.zeros_like(y_scr)

    @pl.when(step % tiles_per_seq == 0)
    def _():
        u_scr[0:SUBLANES, :] = jnp.zeros((SUBLANES, XBC), F32)
        q_scr[0:SUBLANES, :] = jnp.zeros((SUBLANES, d_mix_conv), F32)
        s_scr[...] = jnp.zeros_like(s_scr)

    fillers = [functools.partial(merge_out_proj, j, r)
               for j in range(n_out_blocks) for r in range(2)]
    fillers += [functools.partial(z_proj, j) for j in range(D_SSD // cb_w)]
    n_before_finish = len(fillers)
    fillers += [merge_norm]
    fillers += [functools.partial(merge_gate, j) for j in range(n_out_blocks)]
    fillers += [functools.partial(sc_piece, cblk, i) for cblk in range(col_blocks) for i in range(4)]
    chain = [pre_norm] + [functools.partial(xbc_conv, j) for j in range(XBC // cb_w)]
    chain += [dt_proj, tile_prep]
    for c in range(n_chunks):
        chain += [functools.partial(chunk_prep, c)]
        chain += [functools.partial(chunk_quad, qd) for qd in range(N_HEADS // QUAD)]
        chain += [chunk_finish]

    fillers.reverse()
    for _ in range(max(n_before_finish - chain.index(chunk_finish), 1)):
        fillers.pop()()
    for piece in chain:
        piece()
        if fillers:
            fillers.pop()()
    while fillers:
        fillers.pop()()


def _layer_block(shape, layer):
    return pl.BlockSpec((None,) + tuple(shape), lambda s: (layer,) + (0,) * len(shape),
                        pipeline_mode=pl.Buffered(1))


def _expand_matrix():
    e = np.zeros((DT_PAD, D_SSD), np.float32)
    for h in range(N_HEADS):
        e[h, h * HEAD_DIM:(h + 1) * HEAD_DIM] = 1.0
    return jnp.asarray(np.concatenate([e, e], axis=0), BF16)


def _weight_split_kernel(wt_ref, head_ref, main_ref):
    n_dt = OFF_DT + N_HEADS
    row = lax.broadcasted_iota(jnp.int32, (HEAD_COLS, wt_ref.shape[1]), 0)
    head_t = jnp.where(row < n_dt, wt_ref[0:HEAD_COLS, :], 0.0)
    head_ref[...] = head_t.T.astype(BF16)
    main_ref[...] = wt_ref[n_dt:, :].T.astype(BF16)


def _split_input_weight(w_in):
    depth, d_model, cols = w_in.shape
    main_cols = cols - (OFF_DT + N_HEADS)
    rows = min(WEIGHT_PREP_ROWS, d_model)
    assert d_model % rows == 0
    return pl.pallas_call(
        _weight_split_kernel,
        grid=(depth, d_model // rows),
        in_specs=[pl.BlockSpec((None, cols, rows), lambda l, r: (l, 0, r))],
        out_specs=[pl.BlockSpec((None, rows, HEAD_COLS), lambda l, r: (l, r, 0)),
                   pl.BlockSpec((None, rows, main_cols), lambda l, r: (l, r, 0))],
        out_shape=[jax.ShapeDtypeStruct((depth, d_model, HEAD_COLS), BF16),
                   jax.ShapeDtypeStruct((depth, d_model, main_cols), BF16)],
        name="weight_split",
    )(jnp.swapaxes(w_in, 1, 2))


SEQ_TILE = 512
WEIGHT_PREP_ROWS = 256
VMEM_LIMIT = 58 * 1024 * 1024


def kernel(x, p, norm_pre, norm_post, w_in, ssd_conv_w, ssd_conv_b, dt_bias, a_log, d_skip,
           ssd_norm, sc_conv_w, w_out, w_ple_gate, w_ple_proj):
    depth, bsz, length, ple_dim = p.shape
    d_model = x.shape[-1]
    d_mix_conv = sc_conv_w.shape[-1]
    d_mix = D_SSD + d_mix_conv
    tl = min(SEQ_TILE, length)
    assert length % tl == 0 and tl % CHUNK == 0 and d_mix_conv % SC_COL_BLOCK == 0
    tiles_per_seq = length // tl
    n_tiles = bsz * tiles_per_seq

    w_head, w_main = _split_input_weight(w_in)
    main_cols = w_main.shape[-1]
    pad = ((0, 0), (0, DT_PAD - N_HEADS))
    dtb = jnp.pad(dt_bias, pad)[:, None, :]
    a_row = jnp.pad(-jnp.exp(a_log), pad)[:, None, :]
    dskip_row = jnp.repeat(d_skip, HEAD_DIM, axis=1)[:, None, :]
    w_out_b = w_out.astype(BF16)
    w_gate_b = w_ple_gate.astype(BF16)
    w_proj_b = w_ple_proj.astype(BF16)
    e2 = _expand_matrix()

    def cur_tile(s):
        n = jnp.minimum(s, n_tiles - 1)
        return n // tiles_per_seq, n % tiles_per_seq

    def prev_tile(s):
        n = jnp.maximum(s - 1, 0)
        return n // tiles_per_seq, n % tiles_per_seq

    def ple_tile(layer, s):
        return (layer,) + prev_tile(s) + (0,)

    for layer in range(depth):
        call = pl.pallas_call(
            functools.partial(_layer_kernel, tl=tl, tiles_per_seq=tiles_per_seq,
                              d_mix_conv=d_mix_conv),
            grid=(n_tiles + 1,),
            in_specs=[
                pl.BlockSpec((None, tl, d_model), lambda s: cur_tile(s) + (0,)),
                pl.BlockSpec((None, tl, d_model), lambda s: prev_tile(s) + (0,)),
                pl.BlockSpec((None, None, tl, ple_dim), functools.partial(ple_tile, layer)),
                _layer_block((d_model, HEAD_COLS), layer),
                _layer_block((d_model, main_cols), layer),
                _layer_block((1, d_model), layer),
                _layer_block((SSD_CONV_W, XBC), layer),
                _layer_block((1, XBC), layer),
                _layer_block((1, DT_PAD), layer),
                _layer_block((1, DT_PAD), layer),
                _layer_block((1, D_SSD), layer),
                _layer_block((1, D_SSD), layer),
                _layer_block((SC_CONV_W, d_mix_conv), layer),
                pl.BlockSpec((2 * DT_PAD, D_SSD), lambda s: (0, 0), pipeline_mode=pl.Buffered(1)),
                _layer_block((d_mix, d_model), layer),
                _layer_block((1, d_model), layer),
                _layer_block((d_model, d_model), layer),
                _layer_block((ple_dim, d_model), layer),
            ],
            out_specs=pl.BlockSpec((None, tl, d_model), lambda s: prev_tile(s) + (0,)),
            out_shape=jax.ShapeDtypeStruct((bsz, length, d_model), F32),
            scratch_shapes=[
                pltpu.VMEM((tl, d_model), BF16),
                pltpu.VMEM((tl + SUBLANES, XBC), F32),
                pltpu.VMEM((tl, XBC), F32),
                pltpu.VMEM((tl, DT_PAD), F32),
                pltpu.VMEM((tl, D_SSD), F32),
                pltpu.VMEM((tl + SUBLANES, d_mix_conv), F32),
                pltpu.VMEM((GN, D_SSD), F32),
                pltpu.VMEM((tl, d_mix), BF16),
                pltpu.VMEM((tl, d_model), BF16),
                pltpu.VMEM((tl, DT_PAD), F32),
                pltpu.VMEM((tl, CHUNK), F32),
                pltpu.VMEM((tl, CHUNK), F32),
                pltpu.VMEM((2 * tl, D_SSD), F32),
            ],
            compiler_params=pltpu.CompilerParams(
                dimension_semantics=("arbitrary",), vmem_limit_bytes=VMEM_LIMIT),
            name="layer",
        )
        x = call(x, x, p, w_head, w_main, norm_pre[:, None, :], ssd_conv_w,
                 ssd_conv_b[:, None, :], dtb, a_row, dskip_row, ssd_norm[:, None, :], sc_conv_w,
                 e2, w_out_b, norm_post[:, None, :], w_gate_b, w_proj_b)
    return x
```

```python
import functools

import numpy as np
import jax
import jax.numpy as jnp
from jax import lax
from jax.experimental import pallas as pl
from jax.experimental.pallas import tpu as pltpu

F32 = jnp.float32
BF16 = jnp.bfloat16

EPS = 1e-6
CHUNK = 128
HEAD_DIM = 64
N_HEADS = 16
N_GROUPS = 2
HEADS_PER_GROUP = N_HEADS // N_GROUPS
D_STATE = 64
D_SSD = N_HEADS * HEAD_DIM
GN = N_GROUPS * D_STATE
XBC = D_SSD + 2 * GN
SSD_CONV_W = 4
SC_CONV_W = 3
LANES = 128
SUBLANES = 8
DT_PAD = LANES
QUAD = 4
SC_COL_BLOCK = 256

OFF_XBC = 0
OFF_DT = OFF_XBC + XBC
HEAD_COLS = OFF_DT + DT_PAD


def _sigmoid(v):
    return 0.5 + 0.5 * jnp.tanh(0.5 * v)


def _silu(v):
    h = 0.5 * v
    return h + h * jnp.tanh(h)


def _causal_taps(hist_and_body, taps):
    n_taps = taps.shape[0]
    acc = taps[n_taps - 1:n_taps, :] * hist_and_body[SUBLANES:, :]
    for k in range(n_taps - 1):
        shifted = pltpu.roll(hist_and_body, n_taps - 1 - k, axis=0)
        acc = acc + taps[k:k + 1, :] * shifted[SUBLANES:, :]
    return acc


def _softplus(v):
    return jnp.maximum(v, 0.0) + jnp.log1p(jnp.exp(-jnp.abs(v)))


def _layer_kernel(x_ref, xp_ref, p_ref, wh_ref, w_ref, npre_ref, cw_ref, cb_ref, dtb_ref, a_ref,
                  dskip_ref, snorm_ref, scw_ref, e2_ref, wout_ref, npost_ref, wg_ref, wp_ref,
                  o_ref,
                  hb_scr, u_scr, xbc_scr, dt_scr, z_scr, q_scr, s_scr, y_scr, x1b_scr,
                  acs_scr, acst_scr, dtt_scr, exp_scr,
                  *, tl, tiles_per_seq, d_mix_conv):
    n_chunks = tl // CHUNK
    off_sch = D_SSD
    off_scb = off_sch + d_mix_conv
    off_scc = off_scb + d_mix_conv
    off_zsc = off_scc + d_mix_conv
    step = pl.program_id(0)

    col_blocks = d_mix_conv // SC_COL_BLOCK
    cb_w = SC_COL_BLOCK
    n_out_blocks = o_ref.shape[-1] // cb_w

    def merge_out_proj(j, r):
        rows = slice(r * (tl // 2), (r + 1) * (tl // 2))
        cs = slice(j * cb_w, (j + 1) * cb_w)
        o_ref[rows, cs] = jnp.dot(y_scr[rows, :], wout_ref[:, cs], preferred_element_type=F32)

    def merge_norm():
        mix = o_ref[...]
        msm = jnp.mean(mix * mix, axis=-1, keepdims=True)
        x1 = xp_ref[...] + mix * lax.rsqrt(msm + EPS) * npost_ref[...]
        o_ref[...] = x1
        x1b_scr[...] = x1.astype(BF16)

    def merge_gate(j):
        cs = slice(j * cb_w, (j + 1) * cb_w)
        gate = _sigmoid(jnp.dot(x1b_scr[...], wg_ref[:, cs], preferred_element_type=F32))
        pp = jnp.dot(p_ref[...].astype(BF16), wp_ref[:, cs], preferred_element_type=F32)
        o_ref[:, cs] = o_ref[:, cs] + gate * pp

    def pre_norm():
        x = x_ref[...]
        ms = jnp.mean(x * x, axis=-1, keepdims=True)
        hb_scr[...] = (x * lax.rsqrt(ms + EPS) * npre_ref[...]).astype(BF16)

    def xbc_conv(j):
        cs = slice(j * cb_w, (j + 1) * cb_w)
        u_scr[SUBLANES:SUBLANES + tl, cs] = jnp.dot(hb_scr[...], wh_ref[:, cs],
                                                    preferred_element_type=F32)
        xbc_scr[:, cs] = _silu(cb_ref[:, cs] + _causal_taps(u_scr[:, cs], cw_ref[:, cs]))
        u_scr[0:SUBLANES, cs] = u_scr[tl:tl + SUBLANES, cs]

    def dt_proj():
        dt_raw = jnp.dot(hb_scr[...], wh_ref[:, OFF_DT:OFF_DT + DT_PAD],
                         preferred_element_type=F32)
        dt_scr[...] = _softplus(dt_raw + dtb_ref[...])

    def z_proj(j):
        cs = slice(j * cb_w, (j + 1) * cb_w)
        z_scr[:, cs] = jnp.dot(hb_scr[...], w_ref[:, cs], preferred_element_type=F32)

    def sc_piece(cblk, i):
        c0 = cblk * cb_w
        cs = slice(c0, c0 + cb_w)
        body = slice(SUBLANES, SUBLANES + tl)

        def proj(off):
            return jnp.dot(hb_scr[...], w_ref[:, off + c0:off + c0 + cb_w],
                           preferred_element_type=F32)

        if i == 0:
            q_scr[body, cs] = proj(off_scc)
        elif i == 1:
            q_scr[body, cs] = q_scr[body, cs] * proj(off_sch)
        elif i == 2:
            v = _causal_taps(q_scr[:, cs], scw_ref[:, cs])
            history = q_scr[tl:tl + SUBLANES, cs]
            q_scr[body, cs] = proj(off_scb) * v
            q_scr[0:SUBLANES, cs] = history
        else:
            y_sc = q_scr[body, cs] * _silu(proj(off_zsc))
            y_scr[:, D_SSD + c0:D_SSD + c0 + cb_w] = y_sc.astype(BF16)

    row_i = lax.broadcasted_iota(jnp.int32, (CHUNK, CHUNK), 0)
    col_i = lax.broadcasted_iota(jnp.int32, (CHUNK, CHUNK), 1)
    tril = row_i >= col_i
    tri_b = jnp.where(tril, 1.0, 0.0).astype(BF16)
    tri3 = jnp.concatenate([tri_b, tri_b, tri_b], axis=1)
    lane_q = lax.broadcasted_iota(jnp.int32, (1, QUAD * HEAD_DIM), 1) // HEAD_DIM
    quad_masks = [(lane_q == j).astype(BF16) for j in range(QUAD)]
    s_row = lax.broadcasted_iota(jnp.int32, (GN, D_SSD), 0) // D_STATE
    s_col = lax.broadcasted_iota(jnp.int32, (GN, D_SSD), 1) // (HEADS_PER_GROUP * HEAD_DIM)
    state_mask = s_row == s_col
    ck = {}

    def tile_prep():
        dt = dt_scr[...]
        adt = dt * a_ref[...]
        adt_l = jnp.concatenate([adt[c * CHUNK:(c + 1) * CHUNK] for c in range(n_chunks)], axis=1)
        a1 = adt_l.astype(BF16)
        r1 = adt_l - a1.astype(F32)
        a2 = r1.astype(BF16)
        a3 = (r1 - a2.astype(F32)).astype(BF16)
        acs_l = jnp.dot(tri3, jnp.concatenate([a1, a2, a3], axis=0), preferred_element_type=F32)
        factors = []
        for c in range(n_chunks):
            rs = slice(c * CHUNK, (c + 1) * CHUNK)
            acs = acs_l[:, rs]
            total = acs[CHUNK - 1:CHUNK, :]
            factors += [jnp.exp(total - acs) * dt[rs], jnp.exp(acs)]
            acs_scr[rs, :] = acs
            acst_scr[rs, :] = acs.T
            dtt_scr[rs, :] = dt[rs].T
        both = jnp.concatenate(factors, axis=0)
        hi = both.astype(BF16)
        lo = (both - hi.astype(F32)).astype(BF16)
        exp_scr[...] = jnp.dot(jnp.concatenate([hi, lo], axis=1), e2_ref[...],
                               preferred_element_type=F32)

    def chunk_prep(c):
        rs = slice(c * CHUNK, (c + 1) * CHUNK)
        xs = xbc_scr[rs, 0:D_SSD]
        bm = xbc_scr[rs, D_SSD:D_SSD + GN]
        cm = xbc_scr[rs, D_SSD + GN:XBC]
        bm_b = bm.astype(BF16)
        cbs = []
        for g in range(N_GROUPS):
            cm_g = jnp.where(col_i // D_STATE == g, cm, 0.0).astype(BF16)
            cbs.append(lax.dot_general(cm_g, bm_b, (((1,), (1,)), ((), ())),
                                       preferred_element_type=F32))
        ck.clear()
        ck.update(c=c, rs=rs, xs=xs, bm=bm, cm_b=cm.astype(BF16), xs_b=xs.astype(BF16), cbs=cbs,
                  y_parts=[])

    def chunk_quad(qd):
        c, rs = ck["c"], ck["rs"]
        ms_q = []
        for j in range(QUAD):
            h = qd * QUAD + j
            row = slice(c * CHUNK + h, c * CHUNK + h + 1)
            seg = acs_scr[rs, h:h + 1] - acst_scr[row, :]
            m = jnp.exp(jnp.where(tril, seg, -jnp.inf)) * ck["cbs"][h // HEADS_PER_GROUP]
            ms_q.append((m * dtt_scr[row, :]).astype(BF16))
        lhs = jnp.concatenate(ms_q, axis=1)
        xq = ck["xs_b"][:, qd * QUAD * HEAD_DIM:(qd + 1) * QUAD * HEAD_DIM]
        rhs = jnp.concatenate([xq * quad_masks[j] for j in range(QUAD)], axis=0)
        ck["y_parts"].append(jnp.dot(lhs, rhs, preferred_element_type=F32))

    def chunk_finish():
        c, rs, xs = ck["c"], ck["rs"], ck["xs"]
        wgt_e = exp_scr[2 * c * CHUNK:(2 * c + 1) * CHUNK, :]
        dout_e = exp_scr[(2 * c + 1) * CHUNK:(2 * c + 2) * CHUNK, :]
        y_diag = jnp.concatenate(ck["y_parts"], axis=1)
        s_old = s_scr[...]
        y_off = jnp.dot(ck["cm_b"], s_old.astype(BF16), preferred_element_type=F32)
        y = y_diag + dout_e * y_off + dskip_ref[...] * xs

        xw = (xs * wgt_e).astype(BF16)
        s_chunk = jnp.dot(ck["bm"].T.astype(BF16), xw, preferred_element_type=F32)
        s_scr[...] = s_old * dout_e[CHUNK - 1:CHUNK, :] + jnp.where(state_mask, s_chunk, 0.0)

        yz = y * _silu(z_scr[rs, :])
        gw = D_SSD // N_GROUPS
        for g in range(N_GROUPS):
            seg = yz[:, g * gw:(g + 1) * gw]
            msq = jnp.mean(seg * seg, axis=-1, keepdims=True)
            y_scr[rs, g * gw:(g + 1) * gw] = (
                seg * lax.rsqrt(msq + EPS) * snorm_ref[:, g * gw:(g + 1) * gw]).astype(BF16)

    @pl.when(step == 0)
    def _():
        y_scr[...] = jnp.zeros_like(y_scr)

    @pl.when(step % tiles_per_seq == 0)
    def _():
        u_scr[0:SUBLANES, :] = jnp.zeros((SUBLANES, XBC), F32)
        q_scr[0:SUBLANES, :] = jnp.zeros((SUBLANES, d_mix_conv), F32)
        s_scr[...] = jnp.zeros_like(s_scr)

    out_proj = [functools.partial(merge_out_proj, j, r)
                for j in range(n_out_blocks) for r in range(2)]
    for piece in out_proj[:2]:
        piece()
    for piece in [pre_norm] + [functools.partial(xbc_conv, j) for j in range(XBC // cb_w)]:
        piece()
    dt_proj()
    for piece in out_proj[2:]:
        piece()
    tile_prep()
    for j in range(D_SSD // cb_w):
        z_proj(j)
    merge_norm()
    fillers = [functools.partial(merge_gate, j) for j in range(n_out_blocks)]
    fillers += [functools.partial(sc_piece, cblk, i) for cblk in range(col_blocks) for i in range(4)]
    chain = []
    for c in range(n_chunks):
        chain += [functools.partial(chunk_prep, c)]
        chain += [functools.partial(chunk_quad, qd) for qd in range(N_HEADS // QUAD)]
        chain += [chunk_finish]

    fillers.reverse()
    for piece in chain:
        piece()
        if fillers:
            fillers.pop()()
    while fillers:
        fillers.pop()()


def _layer_block(shape, layer):
    return pl.BlockSpec((None,) + tuple(shape), lambda s: (layer,) + (0,) * len(shape),
                        pipeline_mode=pl.Buffered(1))


def _expand_matrix():
    e = np.zeros((DT_PAD, D_SSD), np.float32)
    for h in range(N_HEADS):
        e[h, h * HEAD_DIM:(h + 1) * HEAD_DIM] = 1.0
    return jnp.asarray(np.concatenate([e, e], axis=0), BF16)


def _weight_split_kernel(wt_ref, head_ref, main_ref):
    n_dt = OFF_DT + N_HEADS
    row = lax.broadcasted_iota(jnp.int32, (HEAD_COLS, wt_ref.shape[1]), 0)
    head_t = jnp.where(row < n_dt, wt_ref[0:HEAD_COLS, :], 0.0)
    head_ref[...] = head_t.T.astype(BF16)
    main_ref[...] = wt_ref[n_dt:, :].T.astype(BF16)


def _split_input_weight(w_in):
    depth, d_model, cols = w_in.shape
    main_cols = cols - (OFF_DT + N_HEADS)
    rows = min(WEIGHT_PREP_ROWS, d_model)
    assert d_model % rows == 0
    return pl.pallas_call(
        _weight_split_kernel,
        grid=(depth, d_model // rows),
        in_specs=[pl.BlockSpec((None, cols, rows), lambda l, r: (l, 0, r))],
        out_specs=[pl.BlockSpec((None, rows, HEAD_COLS), lambda l, r: (l, r, 0)),
                   pl.BlockSpec((None, rows, main_cols), lambda l, r: (l, r, 0))],
        out_shape=[jax.ShapeDtypeStruct((depth, d_model, HEAD_COLS), BF16),
                   jax.ShapeDtypeStruct((depth, d_model, main_cols), BF16)],
        name="weight_split",
    )(jnp.swapaxes(w_in, 1, 2))


SEQ_TILE = 512
WEIGHT_PREP_ROWS = 256
VMEM_LIMIT = 58 * 1024 * 1024


def kernel(x, p, norm_pre, norm_post, w_in, ssd_conv_w, ssd_conv_b, dt_bias, a_log, d_skip,
           ssd_norm, sc_conv_w, w_out, w_ple_gate, w_ple_proj):
    depth, bsz, length, ple_dim = p.shape
    d_model = x.shape[-1]
    d_mix_conv = sc_conv_w.shape[-1]
    d_mix = D_SSD + d_mix_conv
    tl = min(SEQ_TILE, length)
    assert length % tl == 0 and tl % CHUNK == 0 and d_mix_conv % SC_COL_BLOCK == 0
    tiles_per_seq = length // tl
    n_tiles = bsz * tiles_per_seq

    w_head, w_main = _split_input_weight(w_in)
    main_cols = w_main.shape[-1]
    pad = ((0, 0), (0, DT_PAD - N_HEADS))
    dtb = jnp.pad(dt_bias, pad)[:, None, :]
    a_row = jnp.pad(-jnp.exp(a_log), pad)[:, None, :]
    dskip_row = jnp.repeat(d_skip, HEAD_DIM, axis=1)[:, None, :]
    w_out_b = w_out.astype(BF16)
    w_gate_b = w_ple_gate.astype(BF16)
    w_proj_b = w_ple_proj.astype(BF16)
    e2 = _expand_matrix()

    def cur_tile(s):
        n = jnp.minimum(s, n_tiles - 1)
        return n // tiles_per_seq, n % tiles_per_seq

    def prev_tile(s):
        n = jnp.maximum(s - 1, 0)
        return n // tiles_per_seq, n % tiles_per_seq

    def ple_tile(layer, s):
        return (layer,) + prev_tile(s) + (0,)

    for layer in range(depth):
        call = pl.pallas_call(
            functools.partial(_layer_kernel, tl=tl, tiles_per_seq=tiles_per_seq,
                              d_mix_conv=d_mix_conv),
            grid=(n_tiles + 1,),
            in_specs=[
                pl.BlockSpec((None, tl, d_model), lambda s: cur_tile(s) + (0,)),
                pl.BlockSpec((None, tl, d_model), lambda s: prev_tile(s) + (0,)),
                pl.BlockSpec((None, None, tl, ple_dim), functools.partial(ple_tile, layer)),
                _layer_block((d_model, HEAD_COLS), layer),
                _layer_block((d_model, main_cols), layer),
                _layer_block((1, d_model), layer),
                _layer_block((SSD_CONV_W, XBC), layer),
                _layer_block((1, XBC), layer),
                _layer_block((1, DT_PAD), layer),
                _layer_block((1, DT_PAD), layer),
                _layer_block((1, D_SSD), layer),
                _layer_block((1, D_SSD), layer),
                _layer_block((SC_CONV_W, d_mix_conv), layer),
                pl.BlockSpec((2 * DT_PAD, D_SSD), lambda s: (0, 0), pipeline_mode=pl.Buffered(1)),
                _layer_block((d_mix, d_model), layer),
                _layer_block((1, d_model), layer),
                _layer_block((d_model, d_model), layer),
                _layer_block((ple_dim, d_model), layer),
            ],
            out_specs=pl.BlockSpec((None, tl, d_model), lambda s: prev_tile(s) + (0,)),
            out_shape=jax.ShapeDtypeStruct((bsz, length, d_model), F32),
            scratch_shapes=[
                pltpu.VMEM((tl, d_model), BF16),
                pltpu.VMEM((tl + SUBLANES, XBC), F32),
                pltpu.VMEM((tl, XBC), F32),
                pltpu.VMEM((tl, DT_PAD), F32),
                pltpu.VMEM((tl, D_SSD), F32),
                pltpu.VMEM((tl + SUBLANES, d_mix_conv), F32),
                pltpu.VMEM((GN, D_SSD), F32),
                pltpu.VMEM((tl, d_mix), BF16),
                pltpu.VMEM((tl, d_model), BF16),
                pltpu.VMEM((tl, DT_PAD), F32),
                pltpu.VMEM((tl, CHUNK), F32),
                pltpu.VMEM((tl, CHUNK), F32),
                pltpu.VMEM((2 * tl, D_SSD), F32),
            ],
            compiler_params=pltpu.CompilerParams(
                dimension_semantics=("arbitrary",), vmem_limit_bytes=VMEM_LIMIT),
            name="layer",
        )
        x = call(x, x, p, w_head, w_main, norm_pre[:, None, :], ssd_conv_w,
                 ssd_conv_b[:, None, :], dtb, a_row, dskip_row, ssd_norm[:, None, :], sc_conv_w,
                 e2, w_out_b, norm_post[:, None, :], w_gate_b, w_proj_b)
    return x
```

```python
import functools

import jax
import jax.numpy as jnp
from jax import lax
from jax.experimental import pallas as pl
from jax.experimental.pallas import tpu as pltpu

F32 = jnp.float32
BF16 = jnp.bfloat16

EPS = 1e-6
CHUNK = 128
HEAD_DIM = 64
N_HEADS = 16
N_GROUPS = 2
HEADS_PER_GROUP = N_HEADS // N_GROUPS
D_STATE = 64
D_SSD = N_HEADS * HEAD_DIM
GN = N_GROUPS * D_STATE
XBC = D_SSD + 2 * GN
SSD_CONV_W = 4
SC_CONV_W = 3
LANES = 128
SUBLANES = 8
DT_PAD = LANES
QUAD = 4
SC_COL_BLOCK = 256

OFF_XBC = 0
OFF_DT = OFF_XBC + XBC
HEAD_COLS = OFF_DT + DT_PAD


def _sigmoid(v):
    return 0.5 + 0.5 * jnp.tanh(0.5 * v)


def _silu(v):
    h = 0.5 * v
    return h + h * jnp.tanh(h)


def _causal_taps(hist_and_body, taps):
    n_taps = taps.shape[0]
    acc = taps[n_taps - 1:n_taps, :] * hist_and_body[SUBLANES:, :]
    for k in range(n_taps - 1):
        shifted = pltpu.roll(hist_and_body, n_taps - 1 - k, axis=0)
        acc = acc + taps[k:k + 1, :] * shifted[SUBLANES:, :]
    return acc


def _softplus(v):
    return jnp.maximum(v, 0.0) + jnp.log1p(jnp.exp(-jnp.abs(v)))


def _layer_kernel(x_ref, xp_ref, p_ref, wh_ref, w_ref, npre_ref, cw_ref, cb_ref, dtb_ref, a_ref,
                  dskip_ref, snorm_ref, scw_ref, wout_ref, npost_ref, wg_ref, wp_ref,
                  o_ref,
                  hb_scr, u_scr, xbc_scr, dt_scr, z_scr, q_scr, s_scr, y_scr, x1b_scr,
                  acs_scr, acst_scr, dtt_scr, wgt_scr, dout_scr,
                  *, tl, tiles_per_seq, d_mix_conv):
    n_chunks = tl // CHUNK
    off_sch = D_SSD
    off_scb = off_sch + d_mix_conv
    off_scc = off_scb + d_mix_conv
    off_zsc = off_scc + d_mix_conv
    step = pl.program_id(0)

    col_blocks = d_mix_conv // SC_COL_BLOCK
    cb_w = SC_COL_BLOCK
    n_out_blocks = o_ref.shape[-1] // cb_w

    def merge_out_proj(j, r):
        rows = slice(r * (tl // 2), (r + 1) * (tl // 2))
        cs = slice(j * cb_w, (j + 1) * cb_w)
        o_ref[rows, cs] = jnp.dot(y_scr[rows, :], wout_ref[:, cs], preferred_element_type=F32)

    def merge_norm():
        mix = o_ref[...]
        msm = jnp.mean(mix * mix, axis=-1, keepdims=True)
        x1 = xp_ref[...] + mix * lax.rsqrt(msm + EPS) * npost_ref[...]
        o_ref[...] = x1
        x1b_scr[...] = x1.astype(BF16)

    def merge_gate(j):
        cs = slice(j * cb_w, (j + 1) * cb_w)
        gate = _sigmoid(jnp.dot(x1b_scr[...], wg_ref[:, cs], preferred_element_type=F32))
        pp = jnp.dot(p_ref[...].astype(BF16), wp_ref[:, cs], preferred_element_type=F32)
        o_ref[:, cs] = o_ref[:, cs] + gate * pp

    def pre_norm():
        x = x_ref[...]
        ms = jnp.mean(x * x, axis=-1, keepdims=True)
        hb_scr[...] = (x * lax.rsqrt(ms + EPS) * npre_ref[...]).astype(BF16)

    def xbc_conv(j):
        cs = slice(j * cb_w, (j + 1) * cb_w)
        u_scr[SUBLANES:SUBLANES + tl, cs] = jnp.dot(hb_scr[...], wh_ref[:, cs],
                                                    preferred_element_type=F32)
        xbc_scr[:, cs] = _silu(cb_ref[:, cs] + _causal_taps(u_scr[:, cs], cw_ref[:, cs]))
        u_scr[0:SUBLANES, cs] = u_scr[tl:tl + SUBLANES, cs]

    def dt_proj():
        dt_raw = jnp.dot(hb_scr[...], wh_ref[:, OFF_DT:OFF_DT + DT_PAD],
                         preferred_element_type=F32)
        dt_scr[...] = _softplus(dt_raw + dtb_ref[...])

    def z_proj(j):
        cs = slice(j * cb_w, (j + 1) * cb_w)
        z_scr[:, cs] = jnp.dot(hb_scr[...], w_ref[:, cs], preferred_element_type=F32)

    def sc_piece(cblk, i):
        c0 = cblk * cb_w
        cs = slice(c0, c0 + cb_w)
        body = slice(SUBLANES, SUBLANES + tl)

        def proj(off):
            return jnp.dot(hb_scr[...], w_ref[:, off + c0:off + c0 + cb_w],
                           preferred_element_type=F32)

        if i == 0:
            q_scr[body, cs] = proj(off_scc)
        elif i == 1:
            q_scr[body, cs] = q_scr[body, cs] * proj(off_sch)
        elif i == 2:
            v = _causal_taps(q_scr[:, cs], scw_ref[:, cs])
            history = q_scr[tl:tl + SUBLANES, cs]
            q_scr[body, cs] = proj(off_scb) * v
            q_scr[0:SUBLANES, cs] = history
        else:
            y_sc = q_scr[body, cs] * _silu(proj(off_zsc))
            y_scr[:, D_SSD + c0:D_SSD + c0 + cb_w] = y_sc.astype(BF16)

    row_i = lax.broadcasted_iota(jnp.int32, (CHUNK, CHUNK), 0)
    col_i = lax.broadcasted_iota(jnp.int32, (CHUNK, CHUNK), 1)
    tril = row_i >= col_i
    tri_b = jnp.where(tril, 1.0, 0.0).astype(BF16)
    tri3 = jnp.concatenate([tri_b, tri_b, tri_b], axis=1)
    lane_q = lax.broadcasted_iota(jnp.int32, (1, QUAD * HEAD_DIM), 1) // HEAD_DIM
    quad_masks = [(lane_q == j).astype(BF16) for j in range(QUAD)]
    first_head_lanes = lax.broadcasted_iota(jnp.int32, (CHUNK, LANES), 1) < HEAD_DIM
    ck = {}

    def tile_prep():
        dt = dt_scr[...]
        adt = dt * a_ref[...]
        adt_l = jnp.concatenate([adt[c * CHUNK:(c + 1) * CHUNK] for c in range(n_chunks)], axis=1)
        a1 = adt_l.astype(BF16)
        r1 = adt_l - a1.astype(F32)
        a2 = r1.astype(BF16)
        a3 = (r1 - a2.astype(F32)).astype(BF16)
        acs_l = jnp.dot(tri3, jnp.concatenate([a1, a2, a3], axis=0), preferred_element_type=F32)
        for c in range(n_chunks):
            rs = slice(c * CHUNK, (c + 1) * CHUNK)
            acs = acs_l[:, rs]
            total = acs[CHUNK - 1:CHUNK, :]
            wgt_scr[rs, :] = jnp.exp(total - acs) * dt[rs]
            dout_scr[rs, :] = jnp.exp(acs)
            acs_scr[rs, :] = acs
            acst_scr[rs, :] = acs.T
            dtt_scr[rs, :] = dt[rs].T

    def head_lanes(ref, rs):
        tiles = []
        for t in range(D_SSD // LANES):
            even = jnp.broadcast_to(ref[rs, 2 * t:2 * t + 1], (CHUNK, LANES))
            odd = jnp.broadcast_to(ref[rs, 2 * t + 1:2 * t + 2], (CHUNK, LANES))
            tiles.append(jnp.where(first_head_lanes, even, odd))
        return jnp.concatenate(tiles, axis=1)

    def chunk_prep(c):
        rs = slice(c * CHUNK, (c + 1) * CHUNK)
        xs = xbc_scr[rs, 0:D_SSD]
        bm = xbc_scr[rs, D_SSD:D_SSD + GN]
        cm = xbc_scr[rs, D_SSD + GN:XBC]
        bm_b = bm.astype(BF16)
        cbs = []
        for g in range(N_GROUPS):
            cm_g = jnp.where(col_i // D_STATE == g, cm, 0.0).astype(BF16)
            cbs.append(lax.dot_general(cm_g, bm_b, (((1,), (1,)), ((), ())),
                                       preferred_element_type=F32))
        ck.clear()
        ck.update(c=c, rs=rs, xs=xs, bm=bm, cm_b=cm.astype(BF16), xs_b=xs.astype(BF16), cbs=cbs,
                  y_parts=[])

    def chunk_quad(qd):
        c, rs = ck["c"], ck["rs"]
        ms_q = []
        for j in range(QUAD):
            h = qd * QUAD + j
            row = slice(c * CHUNK + h, c * CHUNK + h + 1)
            seg = acs_scr[rs, h:h + 1] - acst_scr[row, :]
            m = jnp.exp(jnp.where(tril, seg, -jnp.inf)) * ck["cbs"][h // HEADS_PER_GROUP]
            ms_q.append((m * dtt_scr[row, :]).astype(BF16))
        lhs = jnp.concatenate(ms_q, axis=1)
        xq = ck["xs_b"][:, qd * QUAD * HEAD_DIM:(qd + 1) * QUAD * HEAD_DIM]
        rhs = jnp.concatenate([xq * quad_masks[j] for j in range(QUAD)], axis=0)
        ck["y_parts"].append(jnp.dot(lhs, rhs, preferred_element_type=F32))

    def chunk_finish():
        c, rs, xs = ck["c"], ck["rs"], ck["xs"]
        wgt_e = head_lanes(wgt_scr, rs)
        dout_e = head_lanes(dout_scr, rs)
        y_diag = jnp.concatenate(ck["y_parts"], axis=1)
        y_off = jnp.dot(ck["cm_b"], s_scr[...].astype(BF16), preferred_element_type=F32)
        y = y_diag + dout_e * y_off + dskip_ref[...] * xs

        xw = (xs * wgt_e).astype(BF16)
        bm_t = ck["bm"].T.astype(BF16)
        gw = D_SSD // N_GROUPS
        for g in range(N_GROUPS):
            rows, cols = slice(g * D_STATE, (g + 1) * D_STATE), slice(g * gw, (g + 1) * gw)
            s_chunk = jnp.dot(bm_t[rows, :], xw[:, cols], preferred_element_type=F32)
            s_scr[rows, cols] = s_scr[rows, cols] * dout_e[CHUNK - 1:CHUNK, cols] + s_chunk

        yz = y * _silu(z_scr[rs, :])
        for g in range(N_GROUPS):
            seg = yz[:, g * gw:(g + 1) * gw]
            msq = jnp.mean(seg * seg, axis=-1, keepdims=True)
            y_scr[rs, g * gw:(g + 1) * gw] = (
                seg * lax.rsqrt(msq + EPS) * snorm_ref[:, g * gw:(g + 1) * gw]).astype(BF16)

    @pl.when(step == 0)
    def _():
        y_scr[...] = jnp.zeros_like(y_scr)

    @pl.when(step % tiles_per_seq == 0)
    def _():
        u_scr[0:SUBLANES, :] = jnp.zeros((SUBLANES, XBC), F32)
        q_scr[0:SUBLANES, :] = jnp.zeros((SUBLANES, d_mix_conv), F32)
        s_scr[...] = jnp.zeros_like(s_scr)

    out_proj = [functools.partial(merge_out_proj, j, r)
                for j in range(n_out_blocks) for r in range(2)]
    for piece in out_proj[:2]:
        piece()
    for piece in [pre_norm] + [functools.partial(xbc_conv, j) for j in range(XBC // cb_w)]:
        piece()
    dt_proj()
    for piece in out_proj[2:]:
        piece()
    tile_prep()
    for j in range(D_SSD // cb_w):
        z_proj(j)
    merge_norm()
    fillers = [functools.partial(merge_gate, j) for j in range(n_out_blocks)]
    fillers += [functools.partial(sc_piece, cblk, i) for cblk in range(col_blocks) for i in range(4)]
    chain = []
    for c in range(n_chunks):
        chain += [functools.partial(chunk_prep, c)]
        chain += [functools.partial(chunk_quad, qd) for qd in range(N_HEADS // QUAD)]
        chain += [chunk_finish]

    fillers.reverse()
    for piece in chain:
        piece()
        if fillers:
            fillers.pop()()
    while fillers:
        fillers.pop()()


def _layer_block(shape, layer):
    return pl.BlockSpec((None,) + tuple(shape), lambda s: (layer,) + (0,) * len(shape),
                        pipeline_mode=pl.Buffered(1))


def _weight_split_kernel(wt_ref, head_ref, main_ref):
    n_dt = OFF_DT + N_HEADS
    row = lax.broadcasted_iota(jnp.int32, (HEAD_COLS, wt_ref.shape[1]), 0)
    head_t = jnp.where(row < n_dt, wt_ref[0:HEAD_COLS, :], 0.0)
    head_ref[...] = head_t.T.astype(BF16)
    main_ref[...] = wt_ref[n_dt:, :].T.astype(BF16)


def _split_input_weight(w_in):
    depth, d_model, cols = w_in.shape
    main_cols = cols - (OFF_DT + N_HEADS)
    rows = min(WEIGHT_PREP_ROWS, d_model)
    assert d_model % rows == 0
    return pl.pallas_call(
        _weight_split_kernel,
        grid=(depth, d_model // rows),
        in_specs=[pl.BlockSpec((None, cols, rows), lambda l, r: (l, 0, r))],
        out_specs=[pl.BlockSpec((None, rows, HEAD_COLS), lambda l, r: (l, r, 0)),
                   pl.BlockSpec((None, rows, main_cols), lambda l, r: (l, r, 0))],
        out_shape=[jax.ShapeDtypeStruct((depth, d_model, HEAD_COLS), BF16),
                   jax.ShapeDtypeStruct((depth, d_model, main_cols), BF16)],
        name="weight_split",
    )(jnp.swapaxes(w_in, 1, 2))


SEQ_TILE = 512
WEIGHT_PREP_ROWS = 256
VMEM_LIMIT = 58 * 1024 * 1024


def kernel(x, p, norm_pre, norm_post, w_in, ssd_conv_w, ssd_conv_b, dt_bias, a_log, d_skip,
           ssd_norm, sc_conv_w, w_out, w_ple_gate, w_ple_proj):
    depth, bsz, length, ple_dim = p.shape
    d_model = x.shape[-1]
    d_mix_conv = sc_conv_w.shape[-1]
    d_mix = D_SSD + d_mix_conv
    tl = min(SEQ_TILE, length)
    assert length % tl == 0 and tl % CHUNK == 0 and d_mix_conv % SC_COL_BLOCK == 0
    tiles_per_seq = length // tl
    n_tiles = bsz * tiles_per_seq

    w_head, w_main = _split_input_weight(w_in)
    main_cols = w_main.shape[-1]
    pad = ((0, 0), (0, DT_PAD - N_HEADS))
    dtb = jnp.pad(dt_bias, pad)[:, None, :]
    a_row = jnp.pad(-jnp.exp(a_log), pad)[:, None, :]
    dskip_row = jnp.repeat(d_skip, HEAD_DIM, axis=1)[:, None, :]
    w_out_b = w_out.astype(BF16)
    w_gate_b = w_ple_gate.astype(BF16)
    w_proj_b = w_ple_proj.astype(BF16)

    def cur_tile(s):
        n = jnp.minimum(s, n_tiles - 1)
        return n // tiles_per_seq, n % tiles_per_seq

    def prev_tile(s):
        n = jnp.maximum(s - 1, 0)
        return n // tiles_per_seq, n % tiles_per_seq

    def ple_tile(layer, s):
        return (layer,) + prev_tile(s) + (0,)

    for layer in range(depth):
        call = pl.pallas_call(
            functools.partial(_layer_kernel, tl=tl, tiles_per_seq=tiles_per_seq,
                              d_mix_conv=d_mix_conv),
            grid=(n_tiles + 1,),
            in_specs=[
                pl.BlockSpec((None, tl, d_model), lambda s: cur_tile(s) + (0,)),
                pl.BlockSpec((None, tl, d_model), lambda s: prev_tile(s) + (0,)),
                pl.BlockSpec((None, None, tl, ple_dim), functools.partial(ple_tile, layer)),
                _layer_block((d_model, HEAD_COLS), layer),
                _layer_block((d_model, main_cols), layer),
                _layer_block((1, d_model), layer),
                _layer_block((SSD_CONV_W, XBC), layer),
                _layer_block((1, XBC), layer),
                _layer_block((1, DT_PAD), layer),
                _layer_block((1, DT_PAD), layer),
                _layer_block((1, D_SSD), layer),
                _layer_block((1, D_SSD), layer),
                _layer_block((SC_CONV_W, d_mix_conv), layer),
                _layer_block((d_mix, d_model), layer),
                _layer_block((1, d_model), layer),
                _layer_block((d_model, d_model), layer),
                _layer_block((ple_dim, d_model), layer),
            ],
            out_specs=pl.BlockSpec((None, tl, d_model), lambda s: prev_tile(s) + (0,)),
            out_shape=jax.ShapeDtypeStruct((bsz, length, d_model), F32),
            scratch_shapes=[
                pltpu.VMEM((tl, d_model), BF16),
                pltpu.VMEM((tl + SUBLANES, XBC), F32),
                pltpu.VMEM((tl, XBC), F32),
                pltpu.VMEM((tl, DT_PAD), F32),
                pltpu.VMEM((tl, D_SSD), F32),
                pltpu.VMEM((tl + SUBLANES, d_mix_conv), F32),
                pltpu.VMEM((GN, D_SSD), F32),
                pltpu.VMEM((tl, d_mix), BF16),
                pltpu.VMEM((tl, d_model), BF16),
                pltpu.VMEM((tl, DT_PAD), F32),
                pltpu.VMEM((tl, CHUNK), F32),
                pltpu.VMEM((tl, CHUNK), F32),
                pltpu.VMEM((tl, DT_PAD), F32),
                pltpu.VMEM((tl, DT_PAD), F32),
            ],
            compiler_params=pltpu.CompilerParams(
                dimension_semantics=("arbitrary",), vmem_limit_bytes=VMEM_LIMIT),
            name="layer",
        )
        x = call(x, x, p, w_head, w_main, norm_pre[:, None, :], ssd_conv_w,
                 ssd_conv_b[:, None, :], dtb, a_row, dskip_row, ssd_norm[:, None, :], sc_conv_w,
                 w_out_b, norm_post[:, None, :], w_gate_b, w_proj_b)
    return x
```

```python
import functools

import jax
import jax.numpy as jnp
from jax import lax
from jax.experimental import pallas as pl
from jax.experimental.pallas import tpu as pltpu

F32 = jnp.float32
BF16 = jnp.bfloat16

EPS = 1e-6
CHUNK = 128
HEAD_DIM = 64
N_HEADS = 16
N_GROUPS = 2
HEADS_PER_GROUP = N_HEADS // N_GROUPS
D_STATE = 64
D_SSD = N_HEADS * HEAD_DIM
GN = N_GROUPS * D_STATE
XBC = D_SSD + 2 * GN
SSD_CONV_W = 4
SC_CONV_W = 3
LANES = 128
SUBLANES = 8
DT_PAD = LANES
QUAD = 4
SC_COL_BLOCK = 256

OFF_XBC = 0
OFF_DT = OFF_XBC + XBC
HEAD_COLS = OFF_DT + DT_PAD


def _sigmoid(v):
    return 0.5 + 0.5 * jnp.tanh(0.5 * v)


def _silu(v):
    h = 0.5 * v
    return h + h * jnp.tanh(h)


def _causal_taps(hist_and_body, taps):
    n_taps = taps.shape[0]
    acc = taps[n_taps - 1:n_taps, :] * hist_and_body[SUBLANES:, :]
    for k in range(n_taps - 1):
        shifted = pltpu.roll(hist_and_body, n_taps - 1 - k, axis=0)
        acc = acc + taps[k:k + 1, :] * shifted[SUBLANES:, :]
    return acc


def _softplus(v):
    return jnp.maximum(v, 0.0) + jnp.log1p(jnp.exp(-jnp.abs(v)))


def _layer_kernel(x_ref, p_ref, wh_ref, w_ref, npre_ref, cw_ref, cb_ref, dtb_ref, a_ref,
                  dskip_ref, snorm_ref, scw_ref, wout_ref, npost_ref, wg_ref, wp_ref,
                  o_ref,
                  hb_scr, u_scr, xbc_scr, dt_scr, z_scr, q_scr, s_scr, y_scr, x1b_scr,
                  acs_scr, acst_scr, dtt_scr, wgt_scr, dout_scr, xp_scr,
                  *, tl, tiles_per_seq, d_mix_conv):
    n_chunks = tl // CHUNK
    off_sch = D_SSD
    off_scb = off_sch + d_mix_conv
    off_scc = off_scb + d_mix_conv
    off_zsc = off_scc + d_mix_conv
    step = pl.program_id(0)

    col_blocks = d_mix_conv // SC_COL_BLOCK
    cb_w = SC_COL_BLOCK
    n_out_blocks = o_ref.shape[-1] // cb_w

    def merge_out_proj(j, r):
        rows = slice(r * (tl // 2), (r + 1) * (tl // 2))
        cs = slice(j * cb_w, (j + 1) * cb_w)
        o_ref[rows, cs] = jnp.dot(y_scr[rows, :], wout_ref[:, cs], preferred_element_type=F32)

    def merge_norm():
        mix = o_ref[...]
        msm = jnp.mean(mix * mix, axis=-1, keepdims=True)
        x1 = xp_scr[...] + mix * lax.rsqrt(msm + EPS) * npost_ref[...]
        o_ref[...] = x1
        x1b_scr[...] = x1.astype(BF16)

    def merge_gate(j):
        cs = slice(j * cb_w, (j + 1) * cb_w)
        gate = _sigmoid(jnp.dot(x1b_scr[...], wg_ref[:, cs], preferred_element_type=F32))
        pp = jnp.dot(p_ref[...].astype(BF16), wp_ref[:, cs], preferred_element_type=F32)
        o_ref[:, cs] = o_ref[:, cs] + gate * pp

    def pre_norm():
        x = x_ref[...]
        ms = jnp.mean(x * x, axis=-1, keepdims=True)
        hb_scr[...] = (x * lax.rsqrt(ms + EPS) * npre_ref[...]).astype(BF16)

    def xbc_conv(j):
        cs = slice(j * cb_w, (j + 1) * cb_w)
        u_scr[SUBLANES:SUBLANES + tl, cs] = jnp.dot(hb_scr[...], wh_ref[:, cs],
                                                    preferred_element_type=F32)
        xbc_scr[:, cs] = _silu(cb_ref[:, cs] + _causal_taps(u_scr[:, cs], cw_ref[:, cs]))
        u_scr[0:SUBLANES, cs] = u_scr[tl:tl + SUBLANES, cs]

    def dt_proj():
        dt_raw = jnp.dot(hb_scr[...], wh_ref[:, OFF_DT:OFF_DT + DT_PAD],
                         preferred_element_type=F32)
        dt_scr[...] = _softplus(dt_raw + dtb_ref[...])

    def z_proj(j):
        cs = slice(j * cb_w, (j + 1) * cb_w)
        z_scr[:, cs] = jnp.dot(hb_scr[...], w_ref[:, cs], preferred_element_type=F32)

    def sc_piece(cblk, i):
        c0 = cblk * cb_w
        cs = slice(c0, c0 + cb_w)
        body = slice(SUBLANES, SUBLANES + tl)

        def proj(off):
            return jnp.dot(hb_scr[...], w_ref[:, off + c0:off + c0 + cb_w],
                           preferred_element_type=F32)

        if i == 0:
            q_scr[body, cs] = proj(off_scc)
        elif i == 1:
            q_scr[body, cs] = q_scr[body, cs] * proj(off_sch)
        elif i == 2:
            v = _causal_taps(q_scr[:, cs], scw_ref[:, cs])
            history = q_scr[tl:tl + SUBLANES, cs]
            q_scr[body, cs] = proj(off_scb) * v
            q_scr[0:SUBLANES, cs] = history
        else:
            y_sc = q_scr[body, cs] * _silu(proj(off_zsc))
            y_scr[:, D_SSD + c0:D_SSD + c0 + cb_w] = y_sc.astype(BF16)

    row_i = lax.broadcasted_iota(jnp.int32, (CHUNK, CHUNK), 0)
    col_i = lax.broadcasted_iota(jnp.int32, (CHUNK, CHUNK), 1)
    tril = row_i >= col_i
    tri_b = jnp.where(tril, 1.0, 0.0).astype(BF16)
    tri3 = jnp.concatenate([tri_b, tri_b, tri_b], axis=1)
    lane_q = lax.broadcasted_iota(jnp.int32, (1, QUAD * HEAD_DIM), 1) // HEAD_DIM
    quad_masks = [(lane_q == j).astype(BF16) for j in range(QUAD)]
    first_head_lanes = lax.broadcasted_iota(jnp.int32, (CHUNK, LANES), 1) < HEAD_DIM
    ck = {}

    def tile_prep():
        dt = dt_scr[...]
        adt = dt * a_ref[...]
        adt_l = jnp.concatenate([adt[c * CHUNK:(c + 1) * CHUNK] for c in range(n_chunks)], axis=1)
        a1 = adt_l.astype(BF16)
        r1 = adt_l - a1.astype(F32)
        a2 = r1.astype(BF16)
        a3 = (r1 - a2.astype(F32)).astype(BF16)
        acs_l = jnp.dot(tri3, jnp.concatenate([a1, a2, a3], axis=0), preferred_element_type=F32)
        for c in range(n_chunks):
            rs = slice(c * CHUNK, (c + 1) * CHUNK)
            acs = acs_l[:, rs]
            total = acs[CHUNK - 1:CHUNK, :]
            wgt_scr[rs, :] = jnp.exp(total - acs) * dt[rs]
            dout_scr[rs, :] = jnp.exp(acs)
            acs_scr[rs, :] = acs
            acst_scr[rs, :] = acs.T
            dtt_scr[rs, :] = dt[rs].T

    def head_lanes(ref, rs):
        tiles = []
        for t in range(D_SSD // LANES):
            even = jnp.broadcast_to(ref[rs, 2 * t:2 * t + 1], (CHUNK, LANES))
            odd = jnp.broadcast_to(ref[rs, 2 * t + 1:2 * t + 2], (CHUNK, LANES))
            tiles.append(jnp.where(first_head_lanes, even, odd))
        return jnp.concatenate(tiles, axis=1)

    def chunk_prep(c):
        rs = slice(c * CHUNK, (c + 1) * CHUNK)
        xs = xbc_scr[rs, 0:D_SSD]
        bm = xbc_scr[rs, D_SSD:D_SSD + GN]
        cm = xbc_scr[rs, D_SSD + GN:XBC]
        bm_b = bm.astype(BF16)
        cbs = []
        for g in range(N_GROUPS):
            cm_g = jnp.where(col_i // D_STATE == g, cm, 0.0).astype(BF16)
            cbs.append(lax.dot_general(cm_g, bm_b, (((1,), (1,)), ((), ())),
                                       preferred_element_type=F32))
        ck.clear()
        ck.update(c=c, rs=rs, xs=xs, bm=bm, cm_b=cm.astype(BF16), xs_b=xs.astype(BF16), cbs=cbs,
                  y_parts=[])

    def chunk_quad(qd):
        c, rs = ck["c"], ck["rs"]
        ms_q = []
        for j in range(QUAD):
            h = qd * QUAD + j
            row = slice(c * CHUNK + h, c * CHUNK + h + 1)
            seg = acs_scr[rs, h:h + 1] - acst_scr[row, :]
            m = jnp.exp(jnp.where(tril, seg, -jnp.inf)) * ck["cbs"][h // HEADS_PER_GROUP]
            ms_q.append((m * dtt_scr[row, :]).astype(BF16))
        lhs = jnp.concatenate(ms_q, axis=1)
        xq = ck["xs_b"][:, qd * QUAD * HEAD_DIM:(qd + 1) * QUAD * HEAD_DIM]
        rhs = jnp.concatenate([xq * quad_masks[j] for j in range(QUAD)], axis=0)
        ck["y_parts"].append(jnp.dot(lhs, rhs, preferred_element_type=F32))

    def chunk_finish():
        c, rs, xs = ck["c"], ck["rs"], ck["xs"]
        wgt_e = head_lanes(wgt_scr, rs)
        dout_e = head_lanes(dout_scr, rs)
        y_diag = jnp.concatenate(ck["y_parts"], axis=1)
        y_off = jnp.dot(ck["cm_b"], s_scr[...].astype(BF16), preferred_element_type=F32)
        y = y_diag + dout_e * y_off + dskip_ref[...] * xs

        xw = (xs * wgt_e).astype(BF16)
        bm_t = ck["bm"].T.astype(BF16)
        gw = D_SSD // N_GROUPS
        for g in range(N_GROUPS):
            rows, cols = slice(g * D_STATE, (g + 1) * D_STATE), slice(g * gw, (g + 1) * gw)
            s_chunk = jnp.dot(bm_t[rows, :], xw[:, cols], preferred_element_type=F32)
            s_scr[rows, cols] = s_scr[rows, cols] * dout_e[CHUNK - 1:CHUNK, cols] + s_chunk

        yz = y * _silu(z_scr[rs, :])
        for g in range(N_GROUPS):
            seg = yz[:, g * gw:(g + 1) * gw]
            msq = jnp.mean(seg * seg, axis=-1, keepdims=True)
            y_scr[rs, g * gw:(g + 1) * gw] = (
                seg * lax.rsqrt(msq + EPS) * snorm_ref[:, g * gw:(g + 1) * gw]).astype(BF16)

    @pl.when(step == 0)
    def _():
        y_scr[...] = jnp.zeros_like(y_scr)
        xp_scr[...] = jnp.zeros_like(xp_scr)

    @pl.when(step % tiles_per_seq == 0)
    def _():
        u_scr[0:SUBLANES, :] = jnp.zeros((SUBLANES, XBC), F32)
        q_scr[0:SUBLANES, :] = jnp.zeros((SUBLANES, d_mix_conv), F32)
        s_scr[...] = jnp.zeros_like(s_scr)

    out_proj = [functools.partial(merge_out_proj, j, r)
                for j in range(n_out_blocks) for r in range(2)]
    for piece in out_proj[:2]:
        piece()
    for piece in [pre_norm] + [functools.partial(xbc_conv, j) for j in range(XBC // cb_w)]:
        piece()
    dt_proj()
    for piece in out_proj[2:]:
        piece()
    tile_prep()
    for j in range(D_SSD // cb_w):
        z_proj(j)
    merge_norm()
    fillers = [functools.partial(merge_gate, j) for j in range(n_out_blocks)]
    fillers += [functools.partial(sc_piece, cblk, i) for cblk in range(col_blocks) for i in range(4)]
    chain = []
    for c in range(n_chunks):
        chain += [functools.partial(chunk_prep, c)]
        chain += [functools.partial(chunk_quad, qd) for qd in range(N_HEADS // QUAD)]
        chain += [chunk_finish]

    fillers.reverse()
    for piece in chain:
        piece()
        if fillers:
            fillers.pop()()
    while fillers:
        fillers.pop()()
    xp_scr[...] = x_ref[...]


def _layer_block(shape, layer):
    return pl.BlockSpec((None,) + tuple(shape), lambda s: (layer,) + (0,) * len(shape),
                        pipeline_mode=pl.Buffered(1))


def _weight_split_kernel(wt_ref, head_ref, main_ref):
    n_dt = OFF_DT + N_HEADS
    row = lax.broadcasted_iota(jnp.int32, (HEAD_COLS, wt_ref.shape[1]), 0)
    head_t = jnp.where(row < n_dt, wt_ref[0:HEAD_COLS, :], 0.0)
    head_ref[...] = head_t.T.astype(BF16)
    main_ref[...] = wt_ref[n_dt:, :].T.astype(BF16)


def _split_input_weight(w_in):
    depth, d_model, cols = w_in.shape
    main_cols = cols - (OFF_DT + N_HEADS)
    rows = min(WEIGHT_PREP_ROWS, d_model)
    assert d_model % rows == 0
    return pl.pallas_call(
        _weight_split_kernel,
        grid=(depth, d_model // rows),
        in_specs=[pl.BlockSpec((None, cols, rows), lambda l, r: (l, 0, r))],
        out_specs=[pl.BlockSpec((None, rows, HEAD_COLS), lambda l, r: (l, r, 0)),
                   pl.BlockSpec((None, rows, main_cols), lambda l, r: (l, r, 0))],
        out_shape=[jax.ShapeDtypeStruct((depth, d_model, HEAD_COLS), BF16),
                   jax.ShapeDtypeStruct((depth, d_model, main_cols), BF16)],
        name="weight_split",
    )(jnp.swapaxes(w_in, 1, 2))


SEQ_TILE = 512
WEIGHT_PREP_ROWS = 256
VMEM_LIMIT = 58 * 1024 * 1024


def kernel(x, p, norm_pre, norm_post, w_in, ssd_conv_w, ssd_conv_b, dt_bias, a_log, d_skip,
           ssd_norm, sc_conv_w, w_out, w_ple_gate, w_ple_proj):
    depth, bsz, length, ple_dim = p.shape
    d_model = x.shape[-1]
    d_mix_conv = sc_conv_w.shape[-1]
    d_mix = D_SSD + d_mix_conv
    tl = min(SEQ_TILE, length)
    assert length % tl == 0 and tl % CHUNK == 0 and d_mix_conv % SC_COL_BLOCK == 0
    tiles_per_seq = length // tl
    n_tiles = bsz * tiles_per_seq

    w_head, w_main = _split_input_weight(w_in)
    main_cols = w_main.shape[-1]
    pad = ((0, 0), (0, DT_PAD - N_HEADS))
    dtb = jnp.pad(dt_bias, pad)[:, None, :]
    a_row = jnp.pad(-jnp.exp(a_log), pad)[:, None, :]
    dskip_row = jnp.repeat(d_skip, HEAD_DIM, axis=1)[:, None, :]
    w_out_b = w_out.astype(BF16)
    w_gate_b = w_ple_gate.astype(BF16)
    w_proj_b = w_ple_proj.astype(BF16)

    def cur_tile(s):
        n = jnp.minimum(s, n_tiles - 1)
        return n // tiles_per_seq, n % tiles_per_seq

    def prev_tile(s):
        n = jnp.maximum(s - 1, 0)
        return n // tiles_per_seq, n % tiles_per_seq

    def ple_tile(layer, s):
        return (layer,) + prev_tile(s) + (0,)

    for layer in range(depth):
        call = pl.pallas_call(
            functools.partial(_layer_kernel, tl=tl, tiles_per_seq=tiles_per_seq,
                              d_mix_conv=d_mix_conv),
            grid=(n_tiles + 1,),
            in_specs=[
                pl.BlockSpec((None, tl, d_model), lambda s: cur_tile(s) + (0,)),
                pl.BlockSpec((None, None, tl, ple_dim), functools.partial(ple_tile, layer)),
                _layer_block((d_model, HEAD_COLS), layer),
                _layer_block((d_model, main_cols), layer),
                _layer_block((1, d_model), layer),
                _layer_block((SSD_CONV_W, XBC), layer),
                _layer_block((1, XBC), layer),
                _layer_block((1, DT_PAD), layer),
                _layer_block((1, DT_PAD), layer),
                _layer_block((1, D_SSD), layer),
                _layer_block((1, D_SSD), layer),
                _layer_block((SC_CONV_W, d_mix_conv), layer),
                _layer_block((d_mix, d_model), layer),
                _layer_block((1, d_model), layer),
                _layer_block((d_model, d_model), layer),
                _layer_block((ple_dim, d_model), layer),
            ],
            out_specs=pl.BlockSpec((None, tl, d_model), lambda s: prev_tile(s) + (0,)),
            out_shape=jax.ShapeDtypeStruct((bsz, length, d_model), F32),
            scratch_shapes=[
                pltpu.VMEM((tl, d_model), BF16),
                pltpu.VMEM((tl + SUBLANES, XBC), F32),
                pltpu.VMEM((tl, XBC), F32),
                pltpu.VMEM((tl, DT_PAD), F32),
                pltpu.VMEM((tl, D_SSD), F32),
                pltpu.VMEM((tl + SUBLANES, d_mix_conv), F32),
                pltpu.VMEM((GN, D_SSD), F32),
                pltpu.VMEM((tl, d_mix), BF16),
                pltpu.VMEM((tl, d_model), BF16),
                pltpu.VMEM((tl, DT_PAD), F32),
                pltpu.VMEM((tl, CHUNK), F32),
                pltpu.VMEM((tl, CHUNK), F32),
                pltpu.VMEM((tl, DT_PAD), F32),
                pltpu.VMEM((tl, DT_PAD), F32),
                pltpu.VMEM((tl, d_model), F32),
            ],
            compiler_params=pltpu.CompilerParams(
                dimension_semantics=("arbitrary",), vmem_limit_bytes=VMEM_LIMIT),
            name="layer",
        )
        x = call(x, p, w_head, w_main, norm_pre[:, None, :], ssd_conv_w,
                 ssd_conv_b[:, None, :], dtb, a_row, dskip_row, ssd_norm[:, None, :], sc_conv_w,
                 w_out_b, norm_post[:, None, :], w_gate_b, w_proj_b)
    return x
```

```python
import functools

import jax
import jax.numpy as jnp
from jax import lax
from jax.experimental import pallas as pl
from jax.experimental.pallas import tpu as pltpu

F32 = jnp.float32
BF16 = jnp.bfloat16

EPS = 1e-6
CHUNK = 128
HEAD_DIM = 64
N_HEADS = 16
N_GROUPS = 2
HEADS_PER_GROUP = N_HEADS // N_GROUPS
D_STATE = 64
D_SSD = N_HEADS * HEAD_DIM
GN = N_GROUPS * D_STATE
XBC = D_SSD + 2 * GN
SSD_CONV_W = 4
SC_CONV_W = 3
LANES = 128
SUBLANES = 8
DT_PAD = LANES
QUAD = 4
SC_COL_BLOCK = 256

OFF_XBC = 0
OFF_DT = OFF_XBC + XBC
HEAD_COLS = OFF_DT + DT_PAD


def _sigmoid(v):
    return 0.5 + 0.5 * jnp.tanh(0.5 * v)


def _silu(v):
    h = 0.5 * v
    return h + h * jnp.tanh(h)


def _causal_taps(hist_and_body, taps):
    n_taps = taps.shape[0]
    acc = taps[n_taps - 1:n_taps, :] * hist_and_body[SUBLANES:, :]
    for k in range(n_taps - 1):
        shifted = pltpu.roll(hist_and_body, n_taps - 1 - k, axis=0)
        acc = acc + taps[k:k + 1, :] * shifted[SUBLANES:, :]
    return acc


def _softplus(v):
    return jnp.maximum(v, 0.0) + jnp.log1p(jnp.exp(-jnp.abs(v)))


def _layer_kernel(x_ref, p_ref, wh_ref, w_ref, npre_ref, cw_ref, cb_ref, dtb_ref, a_ref,
                  dskip_ref, snorm_ref, scw_ref, wout_ref, npost_ref, wg_ref, wp_ref,
                  o_ref,
                  hb_scr, u_scr, xbc_scr, dt_scr, z_scr, q_scr, s_scr, y_scr, x1b_scr,
                  acs_scr, acst_scr, dtt_scr, wgt_scr, dout_scr, xp_scr,
                  *, tl, tiles_per_seq, n_tiles, d_mix_conv):
    n_chunks = tl // CHUNK
    off_sch = D_SSD
    off_scb = off_sch + d_mix_conv
    off_scc = off_scb + d_mix_conv
    off_zsc = off_scc + d_mix_conv
    step = pl.program_id(0)

    col_blocks = d_mix_conv // SC_COL_BLOCK
    cb_w = SC_COL_BLOCK
    n_out_blocks = o_ref.shape[-1] // cb_w

    def merge_out_proj(j, r):
        rows = slice(r * (tl // 2), (r + 1) * (tl // 2))
        cs = slice(j * cb_w, (j + 1) * cb_w)
        o_ref[rows, cs] = jnp.dot(y_scr[rows, :], wout_ref[:, cs], preferred_element_type=F32)

    def merge_norm():
        mix = o_ref[...]
        msm = jnp.mean(mix * mix, axis=-1, keepdims=True)
        x1 = xp_scr[...] + mix * lax.rsqrt(msm + EPS) * npost_ref[...]
        o_ref[...] = x1
        x1b_scr[...] = x1.astype(BF16)

    def merge_gate(j):
        cs = slice(j * cb_w, (j + 1) * cb_w)
        gate = _sigmoid(jnp.dot(x1b_scr[...], wg_ref[:, cs], preferred_element_type=F32))
        pp = jnp.dot(p_ref[...].astype(BF16), wp_ref[:, cs], preferred_element_type=F32)
        o_ref[:, cs] = o_ref[:, cs] + gate * pp

    def pre_norm():
        x = x_ref[...]
        ms = jnp.mean(x * x, axis=-1, keepdims=True)
        hb_scr[...] = (x * lax.rsqrt(ms + EPS) * npre_ref[...]).astype(BF16)

    def xbc_conv(j):
        cs = slice(j * cb_w, (j + 1) * cb_w)
        u_scr[SUBLANES:SUBLANES + tl, cs] = jnp.dot(hb_scr[...], wh_ref[:, cs],
                                                    preferred_element_type=F32)
        xbc_scr[:, cs] = _silu(cb_ref[:, cs] + _causal_taps(u_scr[:, cs], cw_ref[:, cs]))
        u_scr[0:SUBLANES, cs] = u_scr[tl:tl + SUBLANES, cs]

    def dt_proj():
        dt_raw = jnp.dot(hb_scr[...], wh_ref[:, OFF_DT:OFF_DT + DT_PAD],
                         preferred_element_type=F32)
        dt_scr[...] = _softplus(dt_raw + dtb_ref[...])

    def z_proj(j):
        cs = slice(j * cb_w, (j + 1) * cb_w)
        z_scr[:, cs] = jnp.dot(hb_scr[...], w_ref[:, cs], preferred_element_type=F32)

    def sc_piece(cblk, i):
        c0 = cblk * cb_w
        cs = slice(c0, c0 + cb_w)
        body = slice(SUBLANES, SUBLANES + tl)

        def proj(off):
            return jnp.dot(hb_scr[...], w_ref[:, off + c0:off + c0 + cb_w],
                           preferred_element_type=F32)

        if i == 0:
            q_scr[body, cs] = proj(off_scc)
        elif i == 1:
            q_scr[body, cs] = q_scr[body, cs] * proj(off_sch)
        elif i == 2:
            v = _causal_taps(q_scr[:, cs], scw_ref[:, cs])
            history = q_scr[tl:tl + SUBLANES, cs]
            q_scr[body, cs] = proj(off_scb) * v
            q_scr[0:SUBLANES, cs] = history
        else:
            y_sc = q_scr[body, cs] * _silu(proj(off_zsc))
            y_scr[:, D_SSD + c0:D_SSD + c0 + cb_w] = y_sc.astype(BF16)

    row_i = lax.broadcasted_iota(jnp.int32, (CHUNK, CHUNK), 0)
    col_i = lax.broadcasted_iota(jnp.int32, (CHUNK, CHUNK), 1)
    tril = row_i >= col_i
    tri_b = jnp.where(tril, 1.0, 0.0).astype(BF16)
    tri3 = jnp.concatenate([tri_b, tri_b, tri_b], axis=1)
    lane_q = lax.broadcasted_iota(jnp.int32, (1, QUAD * HEAD_DIM), 1) // HEAD_DIM
    quad_masks = [(lane_q == j).astype(BF16) for j in range(QUAD)]
    first_head_lanes = lax.broadcasted_iota(jnp.int32, (CHUNK, LANES), 1) < HEAD_DIM
    ck = {}

    def tile_prep():
        dt = dt_scr[...]
        adt = dt * a_ref[...]
        adt_l = jnp.concatenate([adt[c * CHUNK:(c + 1) * CHUNK] for c in range(n_chunks)], axis=1)
        a1 = adt_l.astype(BF16)
        r1 = adt_l - a1.astype(F32)
        a2 = r1.astype(BF16)
        a3 = (r1 - a2.astype(F32)).astype(BF16)
        acs_l = jnp.dot(tri3, jnp.concatenate([a1, a2, a3], axis=0), preferred_element_type=F32)
        for c in range(n_chunks):
            rs = slice(c * CHUNK, (c + 1) * CHUNK)
            acs = acs_l[:, rs]
            total = acs[CHUNK - 1:CHUNK, :]
            wgt_scr[rs, :] = jnp.exp(total - acs) * dt[rs]
            dout_scr[rs, :] = jnp.exp(acs)
            acs_scr[rs, :] = acs
            acst_scr[rs, :] = acs.T
            dtt_scr[rs, :] = dt[rs].T

    def head_lanes(ref, rs):
        tiles = []
        for t in range(D_SSD // LANES):
            even = jnp.broadcast_to(ref[rs, 2 * t:2 * t + 1], (CHUNK, LANES))
            odd = jnp.broadcast_to(ref[rs, 2 * t + 1:2 * t + 2], (CHUNK, LANES))
            tiles.append(jnp.where(first_head_lanes, even, odd))
        return jnp.concatenate(tiles, axis=1)

    def chunk_prep(c):
        rs = slice(c * CHUNK, (c + 1) * CHUNK)
        xs = xbc_scr[rs, 0:D_SSD]
        bm = xbc_scr[rs, D_SSD:D_SSD + GN]
        cm = xbc_scr[rs, D_SSD + GN:XBC]
        bm_b = bm.astype(BF16)
        cbs = []
        for g in range(N_GROUPS):
            cm_g = jnp.where(col_i // D_STATE == g, cm, 0.0).astype(BF16)
            cbs.append(lax.dot_general(cm_g, bm_b, (((1,), (1,)), ((), ())),
                                       preferred_element_type=F32))
        ck.clear()
        ck.update(c=c, rs=rs, xs=xs, bm=bm, cm_b=cm.astype(BF16), xs_b=xs.astype(BF16), cbs=cbs,
                  y_parts=[])

    def chunk_quad(qd):
        c, rs = ck["c"], ck["rs"]
        ms_q = []
        for j in range(QUAD):
            h = qd * QUAD + j
            row = slice(c * CHUNK + h, c * CHUNK + h + 1)
            seg = acs_scr[rs, h:h + 1] - acst_scr[row, :]
            m = jnp.exp(jnp.where(tril, seg, -jnp.inf)) * ck["cbs"][h // HEADS_PER_GROUP]
            ms_q.append((m * dtt_scr[row, :]).astype(BF16))
        lhs = jnp.concatenate(ms_q, axis=1)
        xq = ck["xs_b"][:, qd * QUAD * HEAD_DIM:(qd + 1) * QUAD * HEAD_DIM]
        rhs = jnp.concatenate([xq * quad_masks[j] for j in range(QUAD)], axis=0)
        ck["y_parts"].append(jnp.dot(lhs, rhs, preferred_element_type=F32))

    def chunk_finish():
        c, rs, xs = ck["c"], ck["rs"], ck["xs"]
        wgt_e = head_lanes(wgt_scr, rs)
        dout_e = head_lanes(dout_scr, rs)
        y_diag = jnp.concatenate(ck["y_parts"], axis=1)
        y_off = jnp.dot(ck["cm_b"], s_scr[...].astype(BF16), preferred_element_type=F32)
        y = y_diag + dout_e * y_off + dskip_ref[...] * xs

        xw = (xs * wgt_e).astype(BF16)
        bm_t = ck["bm"].T.astype(BF16)
        gw = D_SSD // N_GROUPS
        for g in range(N_GROUPS):
            rows, cols = slice(g * D_STATE, (g + 1) * D_STATE), slice(g * gw, (g + 1) * gw)
            s_chunk = jnp.dot(bm_t[rows, :], xw[:, cols], preferred_element_type=F32)
            s_scr[rows, cols] = s_scr[rows, cols] * dout_e[CHUNK - 1:CHUNK, cols] + s_chunk

        yz = y * _silu(z_scr[rs, :])
        for g in range(N_GROUPS):
            seg = yz[:, g * gw:(g + 1) * gw]
            msq = jnp.mean(seg * seg, axis=-1, keepdims=True)
            y_scr[rs, g * gw:(g + 1) * gw] = (
                seg * lax.rsqrt(msq + EPS) * snorm_ref[:, g * gw:(g + 1) * gw]).astype(BF16)

    @pl.when(step % tiles_per_seq == 0)
    def _():
        u_scr[0:SUBLANES, :] = jnp.zeros((SUBLANES, XBC), F32)
        q_scr[0:SUBLANES, :] = jnp.zeros((SUBLANES, d_mix_conv), F32)
        s_scr[...] = jnp.zeros_like(s_scr)

    def emit(mixer, merge):
        out_proj = [functools.partial(merge_out_proj, j, r)
                    for j in range(n_out_blocks) for r in range(2)] if merge else []
        for piece in out_proj[:2]:
            piece()
        if mixer:
            for piece in [pre_norm] + [functools.partial(xbc_conv, j) for j in range(XBC // cb_w)]:
                piece()
            dt_proj()
        for piece in out_proj[2:]:
            piece()
        fillers, chain = [], []
        if mixer:
            tile_prep()
            for j in range(D_SSD // cb_w):
                z_proj(j)
        if merge:
            merge_norm()
            fillers += [functools.partial(merge_gate, j) for j in range(n_out_blocks)]
        if mixer:
            fillers += [functools.partial(sc_piece, cblk, i)
                        for cblk in range(col_blocks) for i in range(4)]
            for c in range(n_chunks):
                chain += [functools.partial(chunk_prep, c)]
                chain += [functools.partial(chunk_quad, qd) for qd in range(N_HEADS // QUAD)]
                chain += [chunk_finish]
        fillers.reverse()
        for piece in chain:
            piece()
            if fillers:
                fillers.pop()()
        while fillers:
            fillers.pop()()
        if mixer:
            xp_scr[...] = x_ref[...]

    @pl.when(step == 0)
    def _():
        o_ref[...] = jnp.zeros_like(o_ref)
        emit(mixer=True, merge=False)

    @pl.when(step == n_tiles)
    def _():
        emit(mixer=False, merge=True)

    @pl.when(jnp.logical_and(step > 0, step < n_tiles))
    def _():
        emit(mixer=True, merge=True)


def _layer_block(shape, layer):
    return pl.BlockSpec((None,) + tuple(shape), lambda s: (layer,) + (0,) * len(shape),
                        pipeline_mode=pl.Buffered(1))


def _weight_split_kernel(wt_ref, head_ref, main_ref):
    n_dt = OFF_DT + N_HEADS
    row = lax.broadcasted_iota(jnp.int32, (HEAD_COLS, wt_ref.shape[1]), 0)
    head_t = jnp.where(row < n_dt, wt_ref[0:HEAD_COLS, :], 0.0)
    head_ref[...] = head_t.T.astype(BF16)
    main_ref[...] = wt_ref[n_dt:, :].T.astype(BF16)


def _split_input_weight(w_in):
    depth, d_model, cols = w_in.shape
    main_cols = cols - (OFF_DT + N_HEADS)
    rows = min(WEIGHT_PREP_ROWS, d_model)
    assert d_model % rows == 0
    return pl.pallas_call(
        _weight_split_kernel,
        grid=(depth, d_model // rows),
        in_specs=[pl.BlockSpec((None, cols, rows), lambda l, r: (l, 0, r))],
        out_specs=[pl.BlockSpec((None, rows, HEAD_COLS), lambda l, r: (l, r, 0)),
                   pl.BlockSpec((None, rows, main_cols), lambda l, r: (l, r, 0))],
        out_shape=[jax.ShapeDtypeStruct((depth, d_model, HEAD_COLS), BF16),
                   jax.ShapeDtypeStruct((depth, d_model, main_cols), BF16)],
        name="weight_split",
    )(jnp.swapaxes(w_in, 1, 2))


SEQ_TILE = 512
WEIGHT_PREP_ROWS = 256
VMEM_LIMIT = 58 * 1024 * 1024


def kernel(x, p, norm_pre, norm_post, w_in, ssd_conv_w, ssd_conv_b, dt_bias, a_log, d_skip,
           ssd_norm, sc_conv_w, w_out, w_ple_gate, w_ple_proj):
    depth, bsz, length, ple_dim = p.shape
    d_model = x.shape[-1]
    d_mix_conv = sc_conv_w.shape[-1]
    d_mix = D_SSD + d_mix_conv
    tl = min(SEQ_TILE, length)
    assert length % tl == 0 and tl % CHUNK == 0 and d_mix_conv % SC_COL_BLOCK == 0
    tiles_per_seq = length // tl
    n_tiles = bsz * tiles_per_seq

    w_head, w_main = _split_input_weight(w_in)
    main_cols = w_main.shape[-1]
    pad = ((0, 0), (0, DT_PAD - N_HEADS))
    dtb = jnp.pad(dt_bias, pad)[:, None, :]
    a_row = jnp.pad(-jnp.exp(a_log), pad)[:, None, :]
    dskip_row = jnp.repeat(d_skip, HEAD_DIM, axis=1)[:, None, :]
    w_out_b = w_out.astype(BF16)
    w_gate_b = w_ple_gate.astype(BF16)
    w_proj_b = w_ple_proj.astype(BF16)

    def cur_tile(s):
        n = jnp.minimum(s, n_tiles - 1)
        return n // tiles_per_seq, n % tiles_per_seq

    def prev_tile(s):
        n = jnp.maximum(s - 1, 0)
        return n // tiles_per_seq, n % tiles_per_seq

    def ple_tile(layer, s):
        return (layer,) + prev_tile(s) + (0,)

    for layer in range(depth):
        call = pl.pallas_call(
            functools.partial(_layer_kernel, tl=tl, tiles_per_seq=tiles_per_seq,
                              n_tiles=n_tiles, d_mix_conv=d_mix_conv),
            grid=(n_tiles + 1,),
            in_specs=[
                pl.BlockSpec((None, tl, d_model), lambda s: cur_tile(s) + (0,)),
                pl.BlockSpec((None, None, tl, ple_dim), functools.partial(ple_tile, layer)),
                _layer_block((d_model, HEAD_COLS), layer),
                _layer_block((d_model, main_cols), layer),
                _layer_block((1, d_model), layer),
                _layer_block((SSD_CONV_W, XBC), layer),
                _layer_block((1, XBC), layer),
                _layer_block((1, DT_PAD), layer),
                _layer_block((1, DT_PAD), layer),
                _layer_block((1, D_SSD), layer),
                _layer_block((1, D_SSD), layer),
                _layer_block((SC_CONV_W, d_mix_conv), layer),
                _layer_block((d_mix, d_model), layer),
                _layer_block((1, d_model), layer),
                _layer_block((d_model, d_model), layer),
                _layer_block((ple_dim, d_model), layer),
            ],
            out_specs=pl.BlockSpec((None, tl, d_model), lambda s: prev_tile(s) + (0,)),
            out_shape=jax.ShapeDtypeStruct((bsz, length, d_model), F32),
            scratch_shapes=[
                pltpu.VMEM((tl, d_model), BF16),
                pltpu.VMEM((tl + SUBLANES, XBC), F32),
                pltpu.VMEM((tl, XBC), F32),
                pltpu.VMEM((tl, DT_PAD), F32),
                pltpu.VMEM((tl, D_SSD), F32),
                pltpu.VMEM((tl + SUBLANES, d_mix_conv), F32),
                pltpu.VMEM((GN, D_SSD), F32),
                pltpu.VMEM((tl, d_mix), BF16),
                pltpu.VMEM((tl, d_model), BF16),
                pltpu.VMEM((tl, DT_PAD), F32),
                pltpu.VMEM((tl, CHUNK), F32),
                pltpu.VMEM((tl, CHUNK), F32),
                pltpu.VMEM((tl, DT_PAD), F32),
                pltpu.VMEM((tl, DT_PAD), F32),
                pltpu.VMEM((tl, d_model), F32),
            ],
            compiler_params=pltpu.CompilerParams(
                dimension_semantics=("arbitrary",), vmem_limit_bytes=VMEM_LIMIT),
            name="layer",
        )
        x = call(x, p, w_head, w_main, norm_pre[:, None, :], ssd_conv_w,
                 ssd_conv_b[:, None, :], dtb, a_row, dskip_row, ssd_norm[:, None, :], sc_conv_w,
                 w_out_b, norm_post[:, None, :], w_gate_b, w_proj_b)
    return x
```

```python
import functools

import jax
import jax.numpy as jnp
from jax import lax
from jax.experimental import pallas as pl
from jax.experimental.pallas import tpu as pltpu

F32 = jnp.float32
BF16 = jnp.bfloat16

EPS = 1e-6
CHUNK = 128
HEAD_DIM = 64
N_HEADS = 16
N_GROUPS = 2
HEADS_PER_GROUP = N_HEADS // N_GROUPS
D_STATE = 64
D_SSD = N_HEADS * HEAD_DIM
GN = N_GROUPS * D_STATE
XBC = D_SSD + 2 * GN
SSD_CONV_W = 4
SC_CONV_W = 3
LANES = 128
SUBLANES = 8
DT_PAD = LANES
QUAD = 4
SC_COL_BLOCK = 256

OFF_XBC = 0
OFF_DT = OFF_XBC + XBC
HEAD_COLS = OFF_DT + DT_PAD


def _sigmoid(v):
    return 0.5 + 0.5 * jnp.tanh(0.5 * v)


def _silu(v):
    h = 0.5 * v
    return h + h * jnp.tanh(h)


def _causal_taps(hist_and_body, taps):
    n_taps = taps.shape[0]
    acc = taps[n_taps - 1:n_taps, :] * hist_and_body[SUBLANES:, :]
    for k in range(n_taps - 1):
        shifted = pltpu.roll(hist_and_body, n_taps - 1 - k, axis=0)
        acc = acc + taps[k:k + 1, :] * shifted[SUBLANES:, :]
    return acc


def _softplus(v):
    return jnp.maximum(v, 0.0) + jnp.log1p(jnp.exp(-jnp.abs(v)))


def _layer_kernel(x_ref, p_ref, wh_ref, w_ref, npre_ref, cw_ref, cb_ref, dtb_ref, a_ref,
                  dskip_ref, snorm_ref, scw_ref, wout_ref, npost_ref, wg_ref, wp_ref,
                  o_ref,
                  hb_scr, u_scr, xbc_scr, dt_scr, z_scr, q_scr, s_scr, y_scr, x1b_scr,
                  acs_scr, acst_scr, dtt_scr, wgt_scr, dout_scr, xp_scr,
                  *, tl, tiles_per_seq, d_mix_conv):
    n_chunks = tl // CHUNK
    off_sch = D_SSD
    off_scb = off_sch + d_mix_conv
    off_scc = off_scb + d_mix_conv
    off_zsc = off_scc + d_mix_conv
    step = pl.program_id(0)

    col_blocks = d_mix_conv // SC_COL_BLOCK
    cb_w = SC_COL_BLOCK
    n_out_blocks = o_ref.shape[-1] // cb_w

    def merge_out_proj(j, r):
        rows = slice(r * (tl // 2), (r + 1) * (tl // 2))
        cs = slice(j * cb_w, (j + 1) * cb_w)
        o_ref[rows, cs] = jnp.dot(y_scr[rows, :], wout_ref[:, cs], preferred_element_type=F32)

    def merge_norm():
        mix = o_ref[...]
        msm = jnp.mean(mix * mix, axis=-1, keepdims=True)
        x1 = xp_scr[...] + mix * lax.rsqrt(msm + EPS) * npost_ref[...]
        o_ref[...] = x1
        x1b_scr[...] = x1.astype(BF16)

    def merge_gate(j):
        cs = slice(j * cb_w, (j + 1) * cb_w)
        gate = _sigmoid(jnp.dot(x1b_scr[...], wg_ref[:, cs], preferred_element_type=F32))
        pp = jnp.dot(p_ref[...].astype(BF16), wp_ref[:, cs], preferred_element_type=F32)
        o_ref[:, cs] = o_ref[:, cs] + gate * pp

    def pre_norm():
        x = x_ref[...]
        ms = jnp.mean(x * x, axis=-1, keepdims=True)
        hb_scr[...] = (x * lax.rsqrt(ms + EPS) * npre_ref[...]).astype(BF16)

    def xbc_conv(j):
        cs = slice(j * cb_w, (j + 1) * cb_w)
        u_scr[SUBLANES:SUBLANES + tl, cs] = jnp.dot(hb_scr[...], wh_ref[:, cs],
                                                    preferred_element_type=F32)
        xbc_scr[:, cs] = _silu(cb_ref[:, cs] + _causal_taps(u_scr[:, cs], cw_ref[:, cs]))
        u_scr[0:SUBLANES, cs] = u_scr[tl:tl + SUBLANES, cs]

    def dt_proj():
        dt_raw = jnp.dot(hb_scr[...], wh_ref[:, OFF_DT:OFF_DT + DT_PAD],
                         preferred_element_type=F32)
        dt_scr[...] = _softplus(dt_raw + dtb_ref[...])

    def z_proj(j):
        cs = slice(j * cb_w, (j + 1) * cb_w)
        z_scr[:, cs] = jnp.dot(hb_scr[...], w_ref[:, cs], preferred_element_type=F32)

    def sc_piece(cblk, i):
        c0 = cblk * cb_w
        cs = slice(c0, c0 + cb_w)
        body = slice(SUBLANES, SUBLANES + tl)

        def proj(off):
            return jnp.dot(hb_scr[...], w_ref[:, off + c0:off + c0 + cb_w],
                           preferred_element_type=F32)

        if i == 0:
            q_scr[body, cs] = proj(off_scc)
        elif i == 1:
            q_scr[body, cs] = q_scr[body, cs] * proj(off_sch)
        elif i == 2:
            v = _causal_taps(q_scr[:, cs], scw_ref[:, cs])
            history = q_scr[tl:tl + SUBLANES, cs]
            q_scr[body, cs] = proj(off_scb) * v
            q_scr[0:SUBLANES, cs] = history
        else:
            y_sc = q_scr[body, cs] * _silu(proj(off_zsc))
            y_scr[:, D_SSD + c0:D_SSD + c0 + cb_w] = y_sc.astype(BF16)

    row_i = lax.broadcasted_iota(jnp.int32, (CHUNK, CHUNK), 0)
    col_i = lax.broadcasted_iota(jnp.int32, (CHUNK, CHUNK), 1)
    tril = row_i >= col_i
    tri_b = jnp.where(tril, 1.0, 0.0).astype(BF16)
    tri3 = jnp.concatenate([tri_b, tri_b, tri_b], axis=1)
    lane_q = lax.broadcasted_iota(jnp.int32, (1, QUAD * HEAD_DIM), 1) // HEAD_DIM
    quad_masks = [(lane_q == j).astype(BF16) for j in range(QUAD)]
    first_head_lanes = lax.broadcasted_iota(jnp.int32, (CHUNK, LANES), 1) < HEAD_DIM
    ck = {}

    def tile_prep():
        dt = dt_scr[...]
        adt = dt * a_ref[...]
        adt_l = jnp.concatenate([adt[c * CHUNK:(c + 1) * CHUNK] for c in range(n_chunks)], axis=1)
        a1 = adt_l.astype(BF16)
        r1 = adt_l - a1.astype(F32)
        a2 = r1.astype(BF16)
        a3 = (r1 - a2.astype(F32)).astype(BF16)
        acs_l = jnp.dot(tri3, jnp.concatenate([a1, a2, a3], axis=0), preferred_element_type=F32)
        for c in range(n_chunks):
            rs = slice(c * CHUNK, (c + 1) * CHUNK)
            acs = acs_l[:, rs]
            total = acs[CHUNK - 1:CHUNK, :]
            wgt_scr[rs, :] = jnp.exp(total - acs) * dt[rs]
            dout_scr[rs, :] = jnp.exp(acs)
            acs_scr[rs, :] = acs
            acst_scr[rs, :] = acs.T
            dtt_scr[rs, :] = dt[rs].T

    def head_lanes(ref, rs):
        tiles = []
        for t in range(D_SSD // LANES):
            even = jnp.broadcast_to(ref[rs, 2 * t:2 * t + 1], (CHUNK, LANES))
            odd = jnp.broadcast_to(ref[rs, 2 * t + 1:2 * t + 2], (CHUNK, LANES))
            tiles.append(jnp.where(first_head_lanes, even, odd))
        return jnp.concatenate(tiles, axis=1)

    def chunk_prep(c):
        rs = slice(c * CHUNK, (c + 1) * CHUNK)
        xs = xbc_scr[rs, 0:D_SSD]
        bm = xbc_scr[rs, D_SSD:D_SSD + GN]
        cm = xbc_scr[rs, D_SSD + GN:XBC]
        bm_b = bm.astype(BF16)
        cbs = []
        for g in range(N_GROUPS):
            cm_g = jnp.where(col_i // D_STATE == g, cm, 0.0).astype(BF16)
            cbs.append(lax.dot_general(cm_g, bm_b, (((1,), (1,)), ((), ())),
                                       preferred_element_type=F32))
        ck.clear()
        ck.update(c=c, rs=rs, xs=xs, bm=bm, cm_b=cm.astype(BF16), xs_b=xs.astype(BF16), cbs=cbs,
                  y_parts=[])

    def chunk_quad(qd):
        c, rs = ck["c"], ck["rs"]
        ms_q = []
        for j in range(QUAD):
            h = qd * QUAD + j
            row = slice(c * CHUNK + h, c * CHUNK + h + 1)
            seg = acs_scr[rs, h:h + 1] - acst_scr[row, :]
            m = jnp.exp(jnp.where(tril, seg, -jnp.inf)) * ck["cbs"][h // HEADS_PER_GROUP]
            ms_q.append((m * dtt_scr[row, :]).astype(BF16))
        lhs = jnp.concatenate(ms_q, axis=1)
        xq = ck["xs_b"][:, qd * QUAD * HEAD_DIM:(qd + 1) * QUAD * HEAD_DIM]
        rhs = jnp.concatenate([xq * quad_masks[j] for j in range(QUAD)], axis=0)
        ck["y_parts"].append(jnp.dot(lhs, rhs, preferred_element_type=F32))

    def chunk_finish():
        c, rs, xs = ck["c"], ck["rs"], ck["xs"]
        wgt_e = head_lanes(wgt_scr, rs)
        dout_e = head_lanes(dout_scr, rs)
        y_diag = jnp.concatenate(ck["y_parts"], axis=1)
        y_off = jnp.dot(ck["cm_b"], s_scr[...].astype(BF16), preferred_element_type=F32)
        y = y_diag + dout_e * y_off + dskip_ref[...] * xs

        xw = (xs * wgt_e).astype(BF16)
        bm_t = ck["bm"].T.astype(BF16)
        gw = D_SSD // N_GROUPS
        for g in range(N_GROUPS):
            rows, cols = slice(g * D_STATE, (g + 1) * D_STATE), slice(g * gw, (g + 1) * gw)
            s_chunk = jnp.dot(bm_t[rows, :], xw[:, cols], preferred_element_type=F32)
            s_scr[rows, cols] = s_scr[rows, cols] * dout_e[CHUNK - 1:CHUNK, cols] + s_chunk

        yz = y * _silu(z_scr[rs, :])
        for g in range(N_GROUPS):
            seg = yz[:, g * gw:(g + 1) * gw]
            msq = jnp.mean(seg * seg, axis=-1, keepdims=True)
            y_scr[rs, g * gw:(g + 1) * gw] = (
                seg * lax.rsqrt(msq + EPS) * snorm_ref[:, g * gw:(g + 1) * gw]).astype(BF16)

    @pl.when(step == 0)
    def _():
        y_scr[...] = jnp.zeros_like(y_scr)
        xp_scr[...] = jnp.zeros_like(xp_scr)

    @pl.when(step % tiles_per_seq == 0)
    def _():
        u_scr[0:SUBLANES, :] = jnp.zeros((SUBLANES, XBC), F32)
        q_scr[0:SUBLANES, :] = jnp.zeros((SUBLANES, d_mix_conv), F32)
        s_scr[...] = jnp.zeros_like(s_scr)

    out_proj = [functools.partial(merge_out_proj, j, r)
                for j in range(n_out_blocks) for r in range(2)]
    for piece in out_proj[:2]:
        piece()
    for piece in [pre_norm] + [functools.partial(xbc_conv, j) for j in range(XBC // cb_w)]:
        piece()
    dt_proj()
    for piece in out_proj[2:]:
        piece()
    tile_prep()
    for j in range(D_SSD // cb_w):
        z_proj(j)
    merge_norm()
    fillers = [functools.partial(merge_gate, j) for j in range(n_out_blocks)]
    fillers += [functools.partial(sc_piece, cblk, i) for cblk in range(col_blocks) for i in range(4)]
    chain = []
    for c in range(n_chunks):
        chain += [functools.partial(chunk_prep, c)]
        chain += [functools.partial(chunk_quad, qd) for qd in range(N_HEADS // QUAD)]
        chain += [chunk_finish]

    fillers.reverse()
    for piece in chain:
        piece()
        if fillers:
            fillers.pop()()
    while fillers:
        fillers.pop()()
    xp_scr[...] = x_ref[...]


def _layer_block(shape, layer):
    return pl.BlockSpec((None,) + tuple(shape), lambda s: (layer,) + (0,) * len(shape),
                        pipeline_mode=pl.Buffered(1))


def _weight_prep_kernel(wt_ref, wout_ref, wgate_ref, wproj_ref,
                        head_ref, main_ref, wout_b_ref, wgate_b_ref, wproj_b_ref):
    n_dt = OFF_DT + N_HEADS
    row = lax.broadcasted_iota(jnp.int32, (HEAD_COLS, wt_ref.shape[1]), 0)
    head_t = jnp.where(row < n_dt, wt_ref[0:HEAD_COLS, :], 0.0)
    head_ref[...] = head_t.T.astype(BF16)
    main_ref[...] = wt_ref[n_dt:, :].T.astype(BF16)
    wout_b_ref[...] = wout_ref[...].astype(BF16)
    wgate_b_ref[...] = wgate_ref[...].astype(BF16)
    wproj_b_ref[...] = wproj_ref[...].astype(BF16)


def _prepare_weights(w_in, w_out, w_gate, w_proj):
    depth, d_model, cols = w_in.shape
    main_cols = cols - (OFF_DT + N_HEADS)
    n_blk = WEIGHT_PREP_BLOCKS
    assert all(w.shape[1] % (SUBLANES * 2 * n_blk) == 0 for w in (w_in, w_out, w_gate, w_proj))

    def rows_of(w):
        return pl.BlockSpec((None, w.shape[1] // n_blk, w.shape[2]), lambda l, r: (l, r, 0))

    def bf16_like(w):
        return jax.ShapeDtypeStruct(w.shape, BF16)

    rows = d_model // n_blk
    return pl.pallas_call(
        _weight_prep_kernel,
        grid=(depth, n_blk),
        in_specs=[pl.BlockSpec((None, cols, rows), lambda l, r: (l, 0, r)),
                  rows_of(w_out), rows_of(w_gate), rows_of(w_proj)],
        out_specs=[pl.BlockSpec((None, rows, HEAD_COLS), lambda l, r: (l, r, 0)),
                   pl.BlockSpec((None, rows, main_cols), lambda l, r: (l, r, 0)),
                   rows_of(w_out), rows_of(w_gate), rows_of(w_proj)],
        out_shape=[jax.ShapeDtypeStruct((depth, d_model, HEAD_COLS), BF16),
                   jax.ShapeDtypeStruct((depth, d_model, main_cols), BF16),
                   bf16_like(w_out), bf16_like(w_gate), bf16_like(w_proj)],
        name="weight_prep",
    )(jnp.swapaxes(w_in, 1, 2), w_out, w_gate, w_proj)


SEQ_TILE = 512
WEIGHT_PREP_BLOCKS = 4
VMEM_LIMIT = 58 * 1024 * 1024


def kernel(x, p, norm_pre, norm_post, w_in, ssd_conv_w, ssd_conv_b, dt_bias, a_log, d_skip,
           ssd_norm, sc_conv_w, w_out, w_ple_gate, w_ple_proj):
    depth, bsz, length, ple_dim = p.shape
    d_model = x.shape[-1]
    d_mix_conv = sc_conv_w.shape[-1]
    d_mix = D_SSD + d_mix_conv
    tl = min(SEQ_TILE, length)
    assert length % tl == 0 and tl % CHUNK == 0 and d_mix_conv % SC_COL_BLOCK == 0
    tiles_per_seq = length // tl
    n_tiles = bsz * tiles_per_seq

    w_head, w_main, w_out_b, w_gate_b, w_proj_b = _prepare_weights(w_in, w_out, w_ple_gate,
                                                                   w_ple_proj)
    main_cols = w_main.shape[-1]
    pad = ((0, 0), (0, DT_PAD - N_HEADS))
    dtb = jnp.pad(dt_bias, pad)[:, None, :]
    a_row = jnp.pad(-jnp.exp(a_log), pad)[:, None, :]
    dskip_row = jnp.repeat(d_skip, HEAD_DIM, axis=1)[:, None, :]

    def cur_tile(s):
        n = jnp.minimum(s, n_tiles - 1)
        return n // tiles_per_seq, n % tiles_per_seq

    def prev_tile(s):
        n = jnp.maximum(s - 1, 0)
        return n // tiles_per_seq, n % tiles_per_seq

    def ple_tile(layer, s):
        return (layer,) + prev_tile(s) + (0,)

    for layer in range(depth):
        call = pl.pallas_call(
            functools.partial(_layer_kernel, tl=tl, tiles_per_seq=tiles_per_seq,
                              d_mix_conv=d_mix_conv),
            grid=(n_tiles + 1,),
            in_specs=[
                pl.BlockSpec((None, tl, d_model), lambda s: cur_tile(s) + (0,)),
                pl.BlockSpec((None, None, tl, ple_dim), functools.partial(ple_tile, layer)),
                _layer_block((d_model, HEAD_COLS), layer),
                _layer_block((d_model, main_cols), layer),
                _layer_block((1, d_model), layer),
                _layer_block((SSD_CONV_W, XBC), layer),
                _layer_block((1, XBC), layer),
                _layer_block((1, DT_PAD), layer),
                _layer_block((1, DT_PAD), layer),
                _layer_block((1, D_SSD), layer),
                _layer_block((1, D_SSD), layer),
                _layer_block((SC_CONV_W, d_mix_conv), layer),
                _layer_block((d_mix, d_model), layer),
                _layer_block((1, d_model), layer),
                _layer_block((d_model, d_model), layer),
                _layer_block((ple_dim, d_model), layer),
            ],
            out_specs=pl.BlockSpec((None, tl, d_model), lambda s: prev_tile(s) + (0,)),
            out_shape=jax.ShapeDtypeStruct((bsz, length, d_model), F32),
            scratch_shapes=[
                pltpu.VMEM((tl, d_model), BF16),
                pltpu.VMEM((tl + SUBLANES, XBC), F32),
                pltpu.VMEM((tl, XBC), F32),
                pltpu.VMEM((tl, DT_PAD), F32),
                pltpu.VMEM((tl, D_SSD), F32),
                pltpu.VMEM((tl + SUBLANES, d_mix_conv), F32),
                pltpu.VMEM((GN, D_SSD), F32),
                pltpu.VMEM((tl, d_mix), BF16),
                pltpu.VMEM((tl, d_model), BF16),
                pltpu.VMEM((tl, DT_PAD), F32),
                pltpu.VMEM((tl, CHUNK), F32),
                pltpu.VMEM((tl, CHUNK), F32),
                pltpu.VMEM((tl, DT_PAD), F32),
                pltpu.VMEM((tl, DT_PAD), F32),
                pltpu.VMEM((tl, d_model), F32),
            ],
            compiler_params=pltpu.CompilerParams(
                dimension_semantics=("arbitrary",), vmem_limit_bytes=VMEM_LIMIT),
            name="layer",
        )
        x = call(x, p, w_head, w_main, norm_pre[:, None, :], ssd_conv_w,
                 ssd_conv_b[:, None, :], dtb, a_row, dskip_row, ssd_norm[:, None, :], sc_conv_w,
                 w_out_b, norm_post[:, None, :], w_gate_b, w_proj_b)
    return x
```

```python
import functools

import jax
import jax.numpy as jnp
from jax import lax
from jax.experimental import pallas as pl
from jax.experimental.pallas import tpu as pltpu

F32 = jnp.float32
BF16 = jnp.bfloat16

EPS = 1e-6
CHUNK = 128
HEAD_DIM = 64
N_HEADS = 16
N_GROUPS = 2
HEADS_PER_GROUP = N_HEADS // N_GROUPS
D_STATE = 64
D_SSD = N_HEADS * HEAD_DIM
GN = N_GROUPS * D_STATE
XBC = D_SSD + 2 * GN
SSD_CONV_W = 4
SC_CONV_W = 3
LANES = 128
SUBLANES = 8
DT_PAD = LANES
QUAD = 4
SC_COL_BLOCK = 256

OFF_XBC = 0
OFF_DT = OFF_XBC + XBC
HEAD_COLS = OFF_DT + DT_PAD


def _sigmoid(v):
    return 0.5 + 0.5 * jnp.tanh(0.5 * v)


def _silu(v):
    h = 0.5 * v
    return h + h * jnp.tanh(h)


def _causal_taps(hist_and_body, taps):
    n_taps = taps.shape[0]
    acc = taps[n_taps - 1:n_taps, :] * hist_and_body[SUBLANES:, :]
    for k in range(n_taps - 1):
        shifted = pltpu.roll(hist_and_body, n_taps - 1 - k, axis=0)
        acc = acc + taps[k:k + 1, :] * shifted[SUBLANES:, :]
    return acc


def _softplus(v):
    return jnp.maximum(v, 0.0) + jnp.log1p(jnp.exp(-jnp.abs(v)))


def _layer_kernel(x_ref, p_ref, wh_ref, w_ref, npre_ref, cw_ref, cb_ref, dtb_ref, a_ref,
                  dskip_ref, snorm_ref, scw_ref, wout_ref, npost_ref, wg_ref, wp_ref,
                  o_ref,
                  hb_scr, u_scr, xbc_scr, dt_scr, z_scr, q_scr, s_scr, y_scr, x1b_scr,
                  acs_scr, acst_scr, dtt_scr, wgt_scr, dout_scr, xp_scr,
                  *, layer, tl, tiles_per_seq, d_mix_conv):
    npre_ref, cb_ref, snorm_ref, npost_ref = (
        r.at[pl.ds(layer, 1)] for r in (npre_ref, cb_ref, snorm_ref, npost_ref))
    n_chunks = tl // CHUNK
    off_sch = D_SSD
    off_scb = off_sch + d_mix_conv
    off_scc = off_scb + d_mix_conv
    off_zsc = off_scc + d_mix_conv
    step = pl.program_id(0)

    col_blocks = d_mix_conv // SC_COL_BLOCK
    cb_w = SC_COL_BLOCK
    n_out_blocks = o_ref.shape[-1] // cb_w

    def merge_out_proj(j, r):
        rows = slice(r * (tl // 2), (r + 1) * (tl // 2))
        cs = slice(j * cb_w, (j + 1) * cb_w)
        o_ref[rows, cs] = jnp.dot(y_scr[rows, :], wout_ref[:, cs], preferred_element_type=F32)

    def merge_norm():
        mix = o_ref[...]
        msm = jnp.mean(mix * mix, axis=-1, keepdims=True)
        x1 = xp_scr[...] + mix * lax.rsqrt(msm + EPS) * npost_ref[...]
        o_ref[...] = x1
        x1b_scr[...] = x1.astype(BF16)

    def merge_gate(j):
        cs = slice(j * cb_w, (j + 1) * cb_w)
        gate = _sigmoid(jnp.dot(x1b_scr[...], wg_ref[:, cs], preferred_element_type=F32))
        pp = jnp.dot(p_ref[...].astype(BF16), wp_ref[:, cs], preferred_element_type=F32)
        o_ref[:, cs] = o_ref[:, cs] + gate * pp

    def pre_norm():
        x = x_ref[...]
        ms = jnp.mean(x * x, axis=-1, keepdims=True)
        hb_scr[...] = (x * lax.rsqrt(ms + EPS) * npre_ref[...]).astype(BF16)

    def xbc_conv(j):
        cs = slice(j * cb_w, (j + 1) * cb_w)
        u_scr[SUBLANES:SUBLANES + tl, cs] = jnp.dot(hb_scr[...], wh_ref[:, cs],
                                                    preferred_element_type=F32)
        xbc_scr[:, cs] = _silu(cb_ref[:, cs] + _causal_taps(u_scr[:, cs], cw_ref[:, cs]))
        u_scr[0:SUBLANES, cs] = u_scr[tl:tl + SUBLANES, cs]

    def dt_proj():
        dt_raw = jnp.dot(hb_scr[...], wh_ref[:, OFF_DT:OFF_DT + DT_PAD],
                         preferred_element_type=F32)
        dt_scr[...] = _softplus(dt_raw + dtb_ref[...])

    def z_proj(j):
        cs = slice(j * cb_w, (j + 1) * cb_w)
        z_scr[:, cs] = jnp.dot(hb_scr[...], w_ref[:, cs], preferred_element_type=F32)

    def sc_piece(cblk, i):
        c0 = cblk * cb_w
        cs = slice(c0, c0 + cb_w)
        body = slice(SUBLANES, SUBLANES + tl)

        def proj(off):
            return jnp.dot(hb_scr[...], w_ref[:, off + c0:off + c0 + cb_w],
                           preferred_element_type=F32)

        if i == 0:
            q_scr[body, cs] = proj(off_scc)
        elif i == 1:
            q_scr[body, cs] = q_scr[body, cs] * proj(off_sch)
        elif i == 2:
            v = _causal_taps(q_scr[:, cs], scw_ref[:, cs])
            history = q_scr[tl:tl + SUBLANES, cs]
            q_scr[body, cs] = proj(off_scb) * v
            q_scr[0:SUBLANES, cs] = history
        else:
            y_sc = q_scr[body, cs] * _silu(proj(off_zsc))
            y_scr[:, D_SSD + c0:D_SSD + c0 + cb_w] = y_sc.astype(BF16)

    row_i = lax.broadcasted_iota(jnp.int32, (CHUNK, CHUNK), 0)
    col_i = lax.broadcasted_iota(jnp.int32, (CHUNK, CHUNK), 1)
    tril = row_i >= col_i
    tri_b = jnp.where(tril, 1.0, 0.0).astype(BF16)
    tri3 = jnp.concatenate([tri_b, tri_b, tri_b], axis=1)
    lane_q = lax.broadcasted_iota(jnp.int32, (1, QUAD * HEAD_DIM), 1) // HEAD_DIM
    quad_masks = [(lane_q == j).astype(BF16) for j in range(QUAD)]
    first_head_lanes = lax.broadcasted_iota(jnp.int32, (CHUNK, LANES), 1) < HEAD_DIM
    ck = {}

    def tile_prep():
        dt = dt_scr[...]
        adt = dt * a_ref[...]
        adt_l = jnp.concatenate([adt[c * CHUNK:(c + 1) * CHUNK] for c in range(n_chunks)], axis=1)
        a1 = adt_l.astype(BF16)
        r1 = adt_l - a1.astype(F32)
        a2 = r1.astype(BF16)
        a3 = (r1 - a2.astype(F32)).astype(BF16)
        acs_l = jnp.dot(tri3, jnp.concatenate([a1, a2, a3], axis=0), preferred_element_type=F32)
        for c in range(n_chunks):
            rs = slice(c * CHUNK, (c + 1) * CHUNK)
            acs = acs_l[:, rs]
            total = acs[CHUNK - 1:CHUNK, :]
            wgt_scr[rs, :] = jnp.exp(total - acs) * dt[rs]
            dout_scr[rs, :] = jnp.exp(acs)
            acs_scr[rs, :] = acs
            acst_scr[rs, :] = acs.T
            dtt_scr[rs, :] = dt[rs].T

    def head_lanes(ref, rs):
        tiles = []
        for t in range(D_SSD // LANES):
            even = jnp.broadcast_to(ref[rs, 2 * t:2 * t + 1], (CHUNK, LANES))
            odd = jnp.broadcast_to(ref[rs, 2 * t + 1:2 * t + 2], (CHUNK, LANES))
            tiles.append(jnp.where(first_head_lanes, even, odd))
        return jnp.concatenate(tiles, axis=1)

    def chunk_prep(c):
        rs = slice(c * CHUNK, (c + 1) * CHUNK)
        xs = xbc_scr[rs, 0:D_SSD]
        bm = xbc_scr[rs, D_SSD:D_SSD + GN]
        cm = xbc_scr[rs, D_SSD + GN:XBC]
        bm_b = bm.astype(BF16)
        cbs = []
        for g in range(N_GROUPS):
            cm_g = jnp.where(col_i // D_STATE == g, cm, 0.0).astype(BF16)
            cbs.append(lax.dot_general(cm_g, bm_b, (((1,), (1,)), ((), ())),
                                       preferred_element_type=F32))
        ck.clear()
        ck.update(c=c, rs=rs, xs=xs, bm=bm, cm_b=cm.astype(BF16), xs_b=xs.astype(BF16), cbs=cbs,
                  y_parts=[])

    def chunk_quad(qd):
        c, rs = ck["c"], ck["rs"]
        ms_q = []
        for j in range(QUAD):
            h = qd * QUAD + j
            row = slice(c * CHUNK + h, c * CHUNK + h + 1)
            seg = acs_scr[rs, h:h + 1] - acst_scr[row, :]
            m = jnp.exp(jnp.where(tril, seg, -jnp.inf)) * ck["cbs"][h // HEADS_PER_GROUP]
            ms_q.append((m * dtt_scr[row, :]).astype(BF16))
        lhs = jnp.concatenate(ms_q, axis=1)
        xq = ck["xs_b"][:, qd * QUAD * HEAD_DIM:(qd + 1) * QUAD * HEAD_DIM]
        rhs = jnp.concatenate([xq * quad_masks[j] for j in range(QUAD)], axis=0)
        ck["y_parts"].append(jnp.dot(lhs, rhs, preferred_element_type=F32))

    def chunk_finish():
        c, rs, xs = ck["c"], ck["rs"], ck["xs"]
        wgt_e = head_lanes(wgt_scr, rs)
        dout_e = head_lanes(dout_scr, rs)
        y_diag = jnp.concatenate(ck["y_parts"], axis=1)
        y_off = jnp.dot(ck["cm_b"], s_scr[...].astype(BF16), preferred_element_type=F32)
        y = y_diag + dout_e * y_off + dskip_ref[...] * xs

        xw = (xs * wgt_e).astype(BF16)
        bm_t = ck["bm"].T.astype(BF16)
        gw = D_SSD // N_GROUPS
        for g in range(N_GROUPS):
            rows, cols = slice(g * D_STATE, (g + 1) * D_STATE), slice(g * gw, (g + 1) * gw)
            s_chunk = jnp.dot(bm_t[rows, :], xw[:, cols], preferred_element_type=F32)
            s_scr[rows, cols] = s_scr[rows, cols] * dout_e[CHUNK - 1:CHUNK, cols] + s_chunk

        yz = y * _silu(z_scr[rs, :])
        for g in range(N_GROUPS):
            seg = yz[:, g * gw:(g + 1) * gw]
            msq = jnp.mean(seg * seg, axis=-1, keepdims=True)
            y_scr[rs, g * gw:(g + 1) * gw] = (
                seg * lax.rsqrt(msq + EPS) * snorm_ref[:, g * gw:(g + 1) * gw]).astype(BF16)

    @pl.when(step == 0)
    def _():
        y_scr[...] = jnp.zeros_like(y_scr)
        xp_scr[...] = jnp.zeros_like(xp_scr)

    @pl.when(step % tiles_per_seq == 0)
    def _():
        u_scr[0:SUBLANES, :] = jnp.zeros((SUBLANES, XBC), F32)
        q_scr[0:SUBLANES, :] = jnp.zeros((SUBLANES, d_mix_conv), F32)
        s_scr[...] = jnp.zeros_like(s_scr)

    out_proj = [functools.partial(merge_out_proj, j, r)
                for j in range(n_out_blocks) for r in range(2)]
    for piece in out_proj[:2]:
        piece()
    for piece in [pre_norm] + [functools.partial(xbc_conv, j) for j in range(XBC // cb_w)]:
        piece()
    dt_proj()
    for piece in out_proj[2:]:
        piece()
    tile_prep()
    for j in range(D_SSD // cb_w):
        z_proj(j)
    merge_norm()
    fillers = [functools.partial(merge_gate, j) for j in range(n_out_blocks)]
    fillers += [functools.partial(sc_piece, cblk, i) for cblk in range(col_blocks) for i in range(4)]
    chain = []
    for c in range(n_chunks):
        chain += [functools.partial(chunk_prep, c)]
        chain += [functools.partial(chunk_quad, qd) for qd in range(N_HEADS // QUAD)]
        chain += [chunk_finish]

    fillers.reverse()
    for piece in chain:
        piece()
        if fillers:
            fillers.pop()()
    while fillers:
        fillers.pop()()
    xp_scr[...] = x_ref[...]


def _layer_block(shape, layer):
    return pl.BlockSpec((None,) + tuple(shape), lambda s: (layer,) + (0,) * len(shape),
                        pipeline_mode=pl.Buffered(1))


def _all_layers(shape):
    return pl.BlockSpec(tuple(shape), lambda s: (0,) * len(shape), pipeline_mode=pl.Buffered(1))


def _weight_prep_kernel(wt_ref, wout_ref, wgate_ref, wproj_ref,
                        head_ref, main_ref, wout_b_ref, wgate_b_ref, wproj_b_ref):
    n_dt = OFF_DT + N_HEADS
    row = lax.broadcasted_iota(jnp.int32, (HEAD_COLS, wt_ref.shape[1]), 0)
    head_t = jnp.where(row < n_dt, wt_ref[0:HEAD_COLS, :], 0.0)
    head_ref[...] = head_t.T.astype(BF16)
    main_ref[...] = wt_ref[n_dt:, :].T.astype(BF16)
    wout_b_ref[...] = wout_ref[...].astype(BF16)
    wgate_b_ref[...] = wgate_ref[...].astype(BF16)
    wproj_b_ref[...] = wproj_ref[...].astype(BF16)


def _prepare_weights(w_in, w_out, w_gate, w_proj):
    depth, d_model, cols = w_in.shape
    main_cols = cols - (OFF_DT + N_HEADS)
    n_blk = WEIGHT_PREP_BLOCKS
    assert all(w.shape[1] % (SUBLANES * 2 * n_blk) == 0 for w in (w_in, w_out, w_gate, w_proj))

    def rows_of(w):
        return pl.BlockSpec((None, w.shape[1] // n_blk, w.shape[2]), lambda l, r: (l, r, 0))

    def bf16_like(w):
        return jax.ShapeDtypeStruct(w.shape, BF16)

    rows = d_model // n_blk
    return pl.pallas_call(
        _weight_prep_kernel,
        grid=(depth, n_blk),
        in_specs=[pl.BlockSpec((None, cols, rows), lambda l, r: (l, 0, r)),
                  rows_of(w_out), rows_of(w_gate), rows_of(w_proj)],
        out_specs=[pl.BlockSpec((None, rows, HEAD_COLS), lambda l, r: (l, r, 0)),
                   pl.BlockSpec((None, rows, main_cols), lambda l, r: (l, r, 0)),
                   rows_of(w_out), rows_of(w_gate), rows_of(w_proj)],
        out_shape=[jax.ShapeDtypeStruct((depth, d_model, HEAD_COLS), BF16),
                   jax.ShapeDtypeStruct((depth, d_model, main_cols), BF16),
                   bf16_like(w_out), bf16_like(w_gate), bf16_like(w_proj)],
        name="weight_prep",
    )(jnp.swapaxes(w_in, 1, 2), w_out, w_gate, w_proj)


SEQ_TILE = 512
WEIGHT_PREP_BLOCKS = 4
VMEM_LIMIT = 58 * 1024 * 1024


def kernel(x, p, norm_pre, norm_post, w_in, ssd_conv_w, ssd_conv_b, dt_bias, a_log, d_skip,
           ssd_norm, sc_conv_w, w_out, w_ple_gate, w_ple_proj):
    depth, bsz, length, ple_dim = p.shape
    d_model = x.shape[-1]
    d_mix_conv = sc_conv_w.shape[-1]
    d_mix = D_SSD + d_mix_conv
    tl = min(SEQ_TILE, length)
    assert length % tl == 0 and tl % CHUNK == 0 and d_mix_conv % SC_COL_BLOCK == 0
    tiles_per_seq = length // tl
    n_tiles = bsz * tiles_per_seq

    w_head, w_main, w_out_b, w_gate_b, w_proj_b = _prepare_weights(w_in, w_out, w_ple_gate,
                                                                   w_ple_proj)
    main_cols = w_main.shape[-1]
    pad = ((0, 0), (0, DT_PAD - N_HEADS))
    dtb = jnp.pad(dt_bias, pad)[:, None, :]
    a_row = jnp.pad(-jnp.exp(a_log), pad)[:, None, :]
    dskip_row = jnp.repeat(d_skip, HEAD_DIM, axis=1)[:, None, :]

    def cur_tile(s):
        n = jnp.minimum(s, n_tiles - 1)
        return n // tiles_per_seq, n % tiles_per_seq

    def prev_tile(s):
        n = jnp.maximum(s - 1, 0)
        return n // tiles_per_seq, n % tiles_per_seq

    def ple_tile(layer, s):
        return (layer,) + prev_tile(s) + (0,)

    for layer in range(depth):
        call = pl.pallas_call(
            functools.partial(_layer_kernel, layer=layer, tl=tl, tiles_per_seq=tiles_per_seq,
                              d_mix_conv=d_mix_conv),
            grid=(n_tiles + 1,),
            in_specs=[
                pl.BlockSpec((None, tl, d_model), lambda s: cur_tile(s) + (0,)),
                pl.BlockSpec((None, None, tl, ple_dim), functools.partial(ple_tile, layer)),
                _layer_block((d_model, HEAD_COLS), layer),
                _layer_block((d_model, main_cols), layer),
                _all_layers((depth, d_model)),
                _layer_block((SSD_CONV_W, XBC), layer),
                _all_layers((depth, XBC)),
                _layer_block((1, DT_PAD), layer),
                _layer_block((1, DT_PAD), layer),
                _layer_block((1, D_SSD), layer),
                _all_layers((depth, D_SSD)),
                _layer_block((SC_CONV_W, d_mix_conv), layer),
                _layer_block((d_mix, d_model), layer),
                _all_layers((depth, d_model)),
                _layer_block((d_model, d_model), layer),
                _layer_block((ple_dim, d_model), layer),
            ],
            out_specs=pl.BlockSpec((None, tl, d_model), lambda s: prev_tile(s) + (0,)),
            out_shape=jax.ShapeDtypeStruct((bsz, length, d_model), F32),
            scratch_shapes=[
                pltpu.VMEM((tl, d_model), BF16),
                pltpu.VMEM((tl + SUBLANES, XBC), F32),
                pltpu.VMEM((tl, XBC), F32),
                pltpu.VMEM((tl, DT_PAD), F32),
                pltpu.VMEM((tl, D_SSD), F32),
                pltpu.VMEM((tl + SUBLANES, d_mix_conv), F32),
                pltpu.VMEM((GN, D_SSD), F32),
                pltpu.VMEM((tl, d_mix), BF16),
                pltpu.VMEM((tl, d_model), BF16),
                pltpu.VMEM((tl, DT_PAD), F32),
                pltpu.VMEM((tl, CHUNK), F32),
                pltpu.VMEM((tl, CHUNK), F32),
                pltpu.VMEM((tl, DT_PAD), F32),
                pltpu.VMEM((tl, DT_PAD), F32),
                pltpu.VMEM((tl, d_model), F32),
            ],
            compiler_params=pltpu.CompilerParams(
                dimension_semantics=("arbitrary",), vmem_limit_bytes=VMEM_LIMIT),
            name="layer",
        )
        x = call(x, p, w_head, w_main, norm_pre, ssd_conv_w, ssd_conv_b, dtb, a_row, dskip_row,
                 ssd_norm, sc_conv_w, w_out_b, norm_post, w_gate_b, w_proj_b)
    return x
```

```python
import functools

import jax
import jax.numpy as jnp
from jax import lax
from jax.experimental import pallas as pl
from jax.experimental.pallas import tpu as pltpu

F32 = jnp.float32
BF16 = jnp.bfloat16

EPS = 1e-6
CHUNK = 128
HEAD_DIM = 64
N_HEADS = 16
N_GROUPS = 2
HEADS_PER_GROUP = N_HEADS // N_GROUPS
D_STATE = 64
D_SSD = N_HEADS * HEAD_DIM
GN = N_GROUPS * D_STATE
XBC = D_SSD + 2 * GN
SSD_CONV_W = 4
SC_CONV_W = 3
LANES = 128
SUBLANES = 8
DT_PAD = LANES
QUAD = 4
SC_COL_BLOCK = 256

OFF_XBC = 0
OFF_DT = OFF_XBC + XBC
HEAD_COLS = OFF_DT + DT_PAD


def _sigmoid(v):
    return 0.5 + 0.5 * jnp.tanh(0.5 * v)


def _silu(v):
    h = 0.5 * v
    return h + h * jnp.tanh(h)


def _causal_taps(hist_and_body, taps):
    n_taps = taps.shape[0]
    acc = taps[n_taps - 1:n_taps, :] * hist_and_body[SUBLANES:, :]
    for k in range(n_taps - 1):
        shifted = pltpu.roll(hist_and_body, n_taps - 1 - k, axis=0)
        acc = acc + taps[k:k + 1, :] * shifted[SUBLANES:, :]
    return acc


def _softplus(v):
    return jnp.maximum(v, 0.0) + jnp.log1p(jnp.exp(-jnp.abs(v)))


def _layer_kernel(x_ref, p_ref, wh_ref, w_ref, npre_ref, cw_ref, cb_ref, dtb_ref, a_ref,
                  dskip_ref, snorm_ref, scw_ref, wout_ref, npost_ref, wg_ref, wp_ref,
                  o_ref,
                  hb_scr, u_scr, xbc_scr, dt_scr, z_scr, q_scr, s_scr, y_scr, x1b_scr,
                  acs_scr, acst_scr, dtt_scr, wgt_scr, dout_scr, xp_scr,
                  *, layer, tl, tiles_per_seq, d_mix_conv):
    npre_ref, cb_ref, snorm_ref, npost_ref = (
        r.at[pl.ds(layer, 1)] for r in (npre_ref, cb_ref, snorm_ref, npost_ref))
    n_chunks = tl // CHUNK
    off_sch = D_SSD
    off_scb = off_sch + d_mix_conv
    off_scc = off_scb + d_mix_conv
    off_zsc = off_scc + d_mix_conv
    step = pl.program_id(0)

    col_blocks = d_mix_conv // SC_COL_BLOCK
    cb_w = SC_COL_BLOCK
    n_out_blocks = o_ref.shape[-1] // cb_w

    def merge_out_proj(j, r):
        rows = slice(r * (tl // 2), (r + 1) * (tl // 2))
        cs = slice(j * cb_w, (j + 1) * cb_w)
        o_ref[rows, cs] = jnp.dot(y_scr[rows, :], wout_ref[:, cs], preferred_element_type=F32)

    def merge_norm():
        for r in range(n_chunks):
            rs = slice(r * CHUNK, (r + 1) * CHUNK)
            mix = o_ref[rs, :]
            msm = jnp.mean(mix * mix, axis=-1, keepdims=True)
            x1 = xp_scr[rs, :] + mix * lax.rsqrt(msm + EPS) * npost_ref[...]
            o_ref[rs, :] = x1
            x1b_scr[rs, :] = x1.astype(BF16)

    def merge_gate(j):
        cs = slice(j * cb_w, (j + 1) * cb_w)
        gate = _sigmoid(jnp.dot(x1b_scr[...], wg_ref[:, cs], preferred_element_type=F32))
        pp = jnp.dot(p_ref[...].astype(BF16), wp_ref[:, cs], preferred_element_type=F32)
        o_ref[:, cs] = o_ref[:, cs] + gate * pp

    def pre_norm():
        for r in range(n_chunks):
            rs = slice(r * CHUNK, (r + 1) * CHUNK)
            x = x_ref[rs, :]
            ms = jnp.mean(x * x, axis=-1, keepdims=True)
            hb_scr[rs, :] = (x * lax.rsqrt(ms + EPS) * npre_ref[...]).astype(BF16)

    def xbc_conv(j):
        cs = slice(j * cb_w, (j + 1) * cb_w)
        u_scr[SUBLANES:SUBLANES + tl, cs] = jnp.dot(hb_scr[...], wh_ref[:, cs],
                                                    preferred_element_type=F32)
        xbc_scr[:, cs] = _silu(cb_ref[:, cs] + _causal_taps(u_scr[:, cs], cw_ref[:, cs]))
        u_scr[0:SUBLANES, cs] = u_scr[tl:tl + SUBLANES, cs]

    def dt_proj():
        dt_raw = jnp.dot(hb_scr[...], wh_ref[:, OFF_DT:OFF_DT + DT_PAD],
                         preferred_element_type=F32)
        dt_scr[...] = _softplus(dt_raw + dtb_ref[...])

    def z_proj(j):
        cs = slice(j * cb_w, (j + 1) * cb_w)
        z_scr[:, cs] = jnp.dot(hb_scr[...], w_ref[:, cs], preferred_element_type=F32)

    def sc_piece(cblk, i):
        c0 = cblk * cb_w
        cs = slice(c0, c0 + cb_w)
        body = slice(SUBLANES, SUBLANES + tl)

        def proj(off):
            return jnp.dot(hb_scr[...], w_ref[:, off + c0:off + c0 + cb_w],
                           preferred_element_type=F32)

        if i == 0:
            q_scr[body, cs] = proj(off_scc)
        elif i == 1:
            q_scr[body, cs] = q_scr[body, cs] * proj(off_sch)
        elif i == 2:
            v = _causal_taps(q_scr[:, cs], scw_ref[:, cs])
            history = q_scr[tl:tl + SUBLANES, cs]
            q_scr[body, cs] = proj(off_scb) * v
            q_scr[0:SUBLANES, cs] = history
        else:
            y_sc = q_scr[body, cs] * _silu(proj(off_zsc))
            y_scr[:, D_SSD + c0:D_SSD + c0 + cb_w] = y_sc.astype(BF16)

    row_i = lax.broadcasted_iota(jnp.int32, (CHUNK, CHUNK), 0)
    col_i = lax.broadcasted_iota(jnp.int32, (CHUNK, CHUNK), 1)
    tril = row_i >= col_i
    tri_b = jnp.where(tril, 1.0, 0.0).astype(BF16)
    tri3 = jnp.concatenate([tri_b, tri_b, tri_b], axis=1)
    lane_q = lax.broadcasted_iota(jnp.int32, (1, QUAD * HEAD_DIM), 1) // HEAD_DIM
    quad_masks = [(lane_q == j).astype(BF16) for j in range(QUAD)]
    first_head_lanes = lax.broadcasted_iota(jnp.int32, (CHUNK, LANES), 1) < HEAD_DIM
    ck = {}

    def tile_prep():
        dt = dt_scr[...]
        adt = dt * a_ref[...]
        adt_l = jnp.concatenate([adt[c * CHUNK:(c + 1) * CHUNK] for c in range(n_chunks)], axis=1)
        a1 = adt_l.astype(BF16)
        r1 = adt_l - a1.astype(F32)
        a2 = r1.astype(BF16)
        a3 = (r1 - a2.astype(F32)).astype(BF16)
        acs_l = jnp.dot(tri3, jnp.concatenate([a1, a2, a3], axis=0), preferred_element_type=F32)
        for c in range(n_chunks):
            rs = slice(c * CHUNK, (c + 1) * CHUNK)
            acs = acs_l[:, rs]
            total = acs[CHUNK - 1:CHUNK, :]
            wgt_scr[rs, :] = jnp.exp(total - acs) * dt[rs]
            dout_scr[rs, :] = jnp.exp(acs)
            acs_scr[rs, :] = acs
            acst_scr[rs, :] = acs.T
            dtt_scr[rs, :] = dt[rs].T

    def head_lanes(ref, rs):
        tiles = []
        for t in range(D_SSD // LANES):
            even = jnp.broadcast_to(ref[rs, 2 * t:2 * t + 1], (CHUNK, LANES))
            odd = jnp.broadcast_to(ref[rs, 2 * t + 1:2 * t + 2], (CHUNK, LANES))
            tiles.append(jnp.where(first_head_lanes, even, odd))
        return jnp.concatenate(tiles, axis=1)

    def chunk_prep(c):
        rs = slice(c * CHUNK, (c + 1) * CHUNK)
        xs = xbc_scr[rs, 0:D_SSD]
        bm = xbc_scr[rs, D_SSD:D_SSD + GN]
        cm = xbc_scr[rs, D_SSD + GN:XBC]
        bm_b = bm.astype(BF16)
        cbs = []
        for g in range(N_GROUPS):
            cm_g = jnp.where(col_i // D_STATE == g, cm, 0.0).astype(BF16)
            cbs.append(lax.dot_general(cm_g, bm_b, (((1,), (1,)), ((), ())),
                                       preferred_element_type=F32))
        ck.clear()
        ck.update(c=c, rs=rs, xs=xs, bm=bm, cm_b=cm.astype(BF16), xs_b=xs.astype(BF16), cbs=cbs,
                  y_parts=[])

    def chunk_quad(qd):
        c, rs = ck["c"], ck["rs"]
        ms_q = []
        for j in range(QUAD):
            h = qd * QUAD + j
            row = slice(c * CHUNK + h, c * CHUNK + h + 1)
            seg = acs_scr[rs, h:h + 1] - acst_scr[row, :]
            m = jnp.exp(jnp.where(tril, seg, -jnp.inf)) * ck["cbs"][h // HEADS_PER_GROUP]
            ms_q.append((m * dtt_scr[row, :]).astype(BF16))
        lhs = jnp.concatenate(ms_q, axis=1)
        xq = ck["xs_b"][:, qd * QUAD * HEAD_DIM:(qd + 1) * QUAD * HEAD_DIM]
        rhs = jnp.concatenate([xq * quad_masks[j] for j in range(QUAD)], axis=0)
        ck["y_parts"].append(jnp.dot(lhs, rhs, preferred_element_type=F32))

    def chunk_finish():
        c, rs, xs = ck["c"], ck["rs"], ck["xs"]
        wgt_e = head_lanes(wgt_scr, rs)
        dout_e = head_lanes(dout_scr, rs)
        y_diag = jnp.concatenate(ck["y_parts"], axis=1)
        y_off = jnp.dot(ck["cm_b"], s_scr[...].astype(BF16), preferred_element_type=F32)
        y = y_diag + dout_e * y_off + dskip_ref[...] * xs

        xw = (xs * wgt_e).astype(BF16)
        bm_t = ck["bm"].T.astype(BF16)
        gw = D_SSD // N_GROUPS
        for g in range(N_GROUPS):
            rows, cols = slice(g * D_STATE, (g + 1) * D_STATE), slice(g * gw, (g + 1) * gw)
            s_chunk = jnp.dot(bm_t[rows, :], xw[:, cols], preferred_element_type=F32)
            s_scr[rows, cols] = s_scr[rows, cols] * dout_e[CHUNK - 1:CHUNK, cols] + s_chunk

        yz = y * _silu(z_scr[rs, :])
        for g in range(N_GROUPS):
            seg = yz[:, g * gw:(g + 1) * gw]
            msq = jnp.mean(seg * seg, axis=-1, keepdims=True)
            y_scr[rs, g * gw:(g + 1) * gw] = (
                seg * lax.rsqrt(msq + EPS) * snorm_ref[:, g * gw:(g + 1) * gw]).astype(BF16)

    @pl.when(step == 0)
    def _():
        y_scr[...] = jnp.zeros_like(y_scr)
        xp_scr[...] = jnp.zeros_like(xp_scr)

    @pl.when(step % tiles_per_seq == 0)
    def _():
        u_scr[0:SUBLANES, :] = jnp.zeros((SUBLANES, XBC), F32)
        q_scr[0:SUBLANES, :] = jnp.zeros((SUBLANES, d_mix_conv), F32)
        s_scr[...] = jnp.zeros_like(s_scr)

    out_proj = [functools.partial(merge_out_proj, j, r)
                for j in range(n_out_blocks) for r in range(2)]
    for piece in out_proj[:2]:
        piece()
    for piece in [pre_norm] + [functools.partial(xbc_conv, j) for j in range(XBC // cb_w)]:
        piece()
    dt_proj()
    for piece in out_proj[2:]:
        piece()
    tile_prep()
    for j in range(D_SSD // cb_w):
        z_proj(j)
    merge_norm()
    fillers = [functools.partial(merge_gate, j) for j in range(n_out_blocks)]
    fillers += [functools.partial(sc_piece, cblk, i) for cblk in range(col_blocks) for i in range(4)]
    chain = []
    for c in range(n_chunks):
        chain += [functools.partial(chunk_prep, c)]
        chain += [functools.partial(chunk_quad, qd) for qd in range(N_HEADS // QUAD)]
        chain += [chunk_finish]

    fillers.reverse()
    for piece in chain:
        piece()
        if fillers:
            fillers.pop()()
    while fillers:
        fillers.pop()()
    xp_scr[...] = x_ref[...]


def _layer_block(shape, layer):
    return pl.BlockSpec((None,) + tuple(shape), lambda s: (layer,) + (0,) * len(shape),
                        pipeline_mode=pl.Buffered(1))


def _all_layers(shape):
    return pl.BlockSpec(tuple(shape), lambda s: (0,) * len(shape), pipeline_mode=pl.Buffered(1))


def _weight_prep_kernel(wt_ref, wout_ref, wgate_ref, wproj_ref,
                        head_ref, main_ref, wout_b_ref, wgate_b_ref, wproj_b_ref):
    n_dt = OFF_DT + N_HEADS
    row = lax.broadcasted_iota(jnp.int32, (HEAD_COLS, wt_ref.shape[1]), 0)
    head_t = jnp.where(row < n_dt, wt_ref[0:HEAD_COLS, :], 0.0)
    head_ref[...] = head_t.T.astype(BF16)
    main_ref[...] = wt_ref[n_dt:, :].T.astype(BF16)
    wout_b_ref[...] = wout_ref[...].astype(BF16)
    wgate_b_ref[...] = wgate_ref[...].astype(BF16)
    wproj_b_ref[...] = wproj_ref[...].astype(BF16)


def _prepare_weights(w_in, w_out, w_gate, w_proj):
    depth, d_model, cols = w_in.shape
    main_cols = cols - (OFF_DT + N_HEADS)
    n_blk = WEIGHT_PREP_BLOCKS
    assert all(w.shape[1] % (SUBLANES * 2 * n_blk) == 0 for w in (w_in, w_out, w_gate, w_proj))

    def rows_of(w):
        return pl.BlockSpec((None, w.shape[1] // n_blk, w.shape[2]), lambda l, r: (l, r, 0))

    def bf16_like(w):
        return jax.ShapeDtypeStruct(w.shape, BF16)

    rows = d_model // n_blk
    return pl.pallas_call(
        _weight_prep_kernel,
        grid=(depth, n_blk),
        in_specs=[pl.BlockSpec((None, cols, rows), lambda l, r: (l, 0, r)),
                  rows_of(w_out), rows_of(w_gate), rows_of(w_proj)],
        out_specs=[pl.BlockSpec((None, rows, HEAD_COLS), lambda l, r: (l, r, 0)),
                   pl.BlockSpec((None, rows, main_cols), lambda l, r: (l, r, 0)),
                   rows_of(w_out), rows_of(w_gate), rows_of(w_proj)],
        out_shape=[jax.ShapeDtypeStruct((depth, d_model, HEAD_COLS), BF16),
                   jax.ShapeDtypeStruct((depth, d_model, main_cols), BF16),
                   bf16_like(w_out), bf16_like(w_gate), bf16_like(w_proj)],
        name="weight_prep",
    )(jnp.swapaxes(w_in, 1, 2), w_out, w_gate, w_proj)


SEQ_TILE = 512
WEIGHT_PREP_BLOCKS = 4
VMEM_LIMIT = 58 * 1024 * 1024


def kernel(x, p, norm_pre, norm_post, w_in, ssd_conv_w, ssd_conv_b, dt_bias, a_log, d_skip,
           ssd_norm, sc_conv_w, w_out, w_ple_gate, w_ple_proj):
    depth, bsz, length, ple_dim = p.shape
    d_model = x.shape[-1]
    d_mix_conv = sc_conv_w.shape[-1]
    d_mix = D_SSD + d_mix_conv
    tl = min(SEQ_TILE, length)
    assert length % tl == 0 and tl % CHUNK == 0 and d_mix_conv % SC_COL_BLOCK == 0
    tiles_per_seq = length // tl
    n_tiles = bsz * tiles_per_seq

    w_head, w_main, w_out_b, w_gate_b, w_proj_b = _prepare_weights(w_in, w_out, w_ple_gate,
                                                                   w_ple_proj)
    main_cols = w_main.shape[-1]
    pad = ((0, 0), (0, DT_PAD - N_HEADS))
    dtb = jnp.pad(dt_bias, pad)[:, None, :]
    a_row = jnp.pad(-jnp.exp(a_log), pad)[:, None, :]
    dskip_row = jnp.repeat(d_skip, HEAD_DIM, axis=1)[:, None, :]

    def cur_tile(s):
        n = jnp.minimum(s, n_tiles - 1)
        return n // tiles_per_seq, n % tiles_per_seq

    def prev_tile(s):
        n = jnp.maximum(s - 1, 0)
        return n // tiles_per_seq, n % tiles_per_seq

    def ple_tile(layer, s):
        return (layer,) + prev_tile(s) + (0,)

    for layer in range(depth):
        call = pl.pallas_call(
            functools.partial(_layer_kernel, layer=layer, tl=tl, tiles_per_seq=tiles_per_seq,
                              d_mix_conv=d_mix_conv),
            grid=(n_tiles + 1,),
            in_specs=[
                pl.BlockSpec((None, tl, d_model), lambda s: cur_tile(s) + (0,)),
                pl.BlockSpec((None, None, tl, ple_dim), functools.partial(ple_tile, layer)),
                _layer_block((d_model, HEAD_COLS), layer),
                _layer_block((d_model, main_cols), layer),
                _all_layers((depth, d_model)),
                _layer_block((SSD_CONV_W, XBC), layer),
                _all_layers((depth, XBC)),
                _layer_block((1, DT_PAD), layer),
                _layer_block((1, DT_PAD), layer),
                _layer_block((1, D_SSD), layer),
                _all_layers((depth, D_SSD)),
                _layer_block((SC_CONV_W, d_mix_conv), layer),
                _layer_block((d_mix, d_model), layer),
                _all_layers((depth, d_model)),
                _layer_block((d_model, d_model), layer),
                _layer_block((ple_dim, d_model), layer),
            ],
            out_specs=pl.BlockSpec((None, tl, d_model), lambda s: prev_tile(s) + (0,)),
            out_shape=jax.ShapeDtypeStruct((bsz, length, d_model), F32),
            scratch_shapes=[
                pltpu.VMEM((tl, d_model), BF16),
                pltpu.VMEM((tl + SUBLANES, XBC), F32),
                pltpu.VMEM((tl, XBC), F32),
                pltpu.VMEM((tl, DT_PAD), F32),
                pltpu.VMEM((tl, D_SSD), F32),
                pltpu.VMEM((tl + SUBLANES, d_mix_conv), F32),
                pltpu.VMEM((GN, D_SSD), F32),
                pltpu.VMEM((tl, d_mix), BF16),
                pltpu.VMEM((tl, d_model), BF16),
                pltpu.VMEM((tl, DT_PAD), F32),
                pltpu.VMEM((tl, CHUNK), F32),
                pltpu.VMEM((tl, CHUNK), F32),
                pltpu.VMEM((tl, DT_PAD), F32),
                pltpu.VMEM((tl, DT_PAD), F32),
                pltpu.VMEM((tl, d_model), F32),
            ],
            compiler_params=pltpu.CompilerParams(
                dimension_semantics=("arbitrary",), vmem_limit_bytes=VMEM_LIMIT),
            name="layer",
        )
        x = call(x, p, w_head, w_main, norm_pre, ssd_conv_w, ssd_conv_b, dtb, a_row, dskip_row,
                 ssd_norm, sc_conv_w, w_out_b, norm_post, w_gate_b, w_proj_b)
    return x
```
